```python
import math
import jax, jax.numpy as jnp
from jax import lax
import numpy as np

D_MODEL = 2048
BATCH = 4
SEQ = 4096
DEPTH = 1

CHUNK = 128
A_GROUP_DIM = 128
A_GROUPS = (D_MODEL // 2) // A_GROUP_DIM
A_WIDTH = A_GROUPS * A_GROUP_DIM
HEAD_DIM = 64
B_HEADS = (D_MODEL // 2) // HEAD_DIM
B_KV_HEADS = 2
Q_PER_KV = B_HEADS // B_KV_HEADS
B_WIDTH = B_HEADS * HEAD_DIM
KV_WIDTH = B_KV_HEADS * HEAD_DIM
WINDOW = 128
BLOCK = 128
N_BUCKETS = 32
MAX_DISTANCE = 128
MIX_WIDTH = A_WIDTH + B_WIDTH
PROJ_WIDTH = 2 * A_WIDTH + B_WIDTH + 2 * KV_WIDTH
SPLITS = [A_WIDTH, 2 * A_WIDTH, 2 * A_WIDTH + B_WIDTH, 2 * A_WIDTH + B_WIDTH + KV_WIDTH]
D_FF = 4 * D_MODEL
EPS = 1e-5
NEG = -1e30

kernel_name = "hymba_gmlp_swa_sink_t5_sqrelu"


def rms_norm(x, g):
    xf = x.astype(jnp.float32)
    y = xf * lax.rsqrt(jnp.mean(xf * xf, axis=-1, keepdims=True) + EPS)
    return (y * g.astype(jnp.float32)).astype(x.dtype)


def layer_norm(x, g, b):
    xf = x.astype(jnp.float32)
    mu = jnp.mean(xf, axis=-1, keepdims=True)
    xc = xf - mu
    var = jnp.mean(xc * xc, axis=-1, keepdims=True)
    y = xc * lax.rsqrt(var + EPS) * g.astype(jnp.float32) + b.astype(jnp.float32)
    return y.astype(x.dtype)


def t5_relative_bias(table):
    i = jnp.arange(BLOCK)[:, None]
    j = jnp.arange(2 * BLOCK)[None, :]
    rel = jnp.maximum(i + BLOCK - j, 0)
    n_exact = N_BUCKETS // 2
    relf = jnp.maximum(rel, n_exact).astype(jnp.float32)
    large = n_exact + (jnp.log(relf / n_exact) / math.log(MAX_DISTANCE / n_exact)
                       * (N_BUCKETS - n_exact)).astype(jnp.int32)
    large = jnp.minimum(large, N_BUCKETS - 1)
    bucket = jnp.where(rel < n_exact, rel, large)
    return jnp.transpose(table.astype(jnp.float32)[bucket], (2, 0, 1))


def spatial_gating(u, v, ln_g, ln_b, w_s, b_s):
    Bsz, S = u.shape[:2]
    u = jax.nn.gelu(u)
    v = layer_norm(jax.nn.gelu(v), ln_g, ln_b)
    v = v.reshape(Bsz, S // CHUNK, CHUNK, A_GROUPS, A_GROUP_DIM)
    causal = jnp.tril(jnp.ones((CHUNK, CHUNK), dtype=bool))
    w = jnp.where(causal[None], w_s, jnp.zeros_like(w_s))
    mixed = jnp.einsum('gts,bcsgd->bctgd', w, v) + jnp.transpose(b_s)[None, None, :, :, None]
    return u * mixed.reshape(Bsz, S, A_GROUPS, A_GROUP_DIM)


def sliding_window_attention(q, k, v, sinks, rel_bias):
    Bsz, S = q.shape[:2]
    nb = S // BLOCK
    qb = q.reshape(Bsz, nb, BLOCK, B_KV_HEADS, Q_PER_KV, HEAD_DIM)

    def band(t):
        tp = jnp.pad(t, ((0, 0), (BLOCK, 0), (0, 0), (0, 0)))
        tp = tp.reshape(Bsz, nb + 1, BLOCK, B_KV_HEADS, HEAD_DIM)
        return jnp.concatenate([tp[:, :-1], tp[:, 1:]], axis=2)

    kb, vb = band(k), band(v)
    s = jnp.einsum('bnikgd,bnjkd->bnkgij', qb, kb).astype(jnp.float32) * (HEAD_DIM ** -0.5)
    s = s + rel_bias.reshape(B_KV_HEADS, Q_PER_KV, BLOCK, 2 * BLOCK)
    i = jnp.arange(BLOCK)[:, None]
    j = jnp.arange(2 * BLOCK)[None, :]
    rel = i + BLOCK - j
    in_window = (rel >= 0) & (rel < WINDOW)
    key_exists = (jnp.arange(nb)[:, None] > 0) | (jnp.arange(2 * BLOCK)[None, :] >= BLOCK)
    mask = in_window[None] & key_exists[:, None, :]
    s = jnp.where(mask[None, :, None, None], s, NEG)
    sink = sinks.astype(jnp.float32).reshape(B_KV_HEADS, Q_PER_KV)[None, None, :, :, None, None]
    m = jnp.maximum(jnp.max(s, axis=-1, keepdims=True), sink)
    p = jnp.exp(s - m)
    denom = jnp.sum(p, axis=-1, keepdims=True) + jnp.exp(sink - m)
    o = jnp.einsum('bnkgij,bnjkd->bnikgd', (p / denom).astype(v.dtype), vb)
    return o.reshape(Bsz, S, B_WIDTH)


def setup_inputs(seed: int = 0) -> dict:
    key = jax.random.key(seed)
    ks = jax.random.split(key, 17)
    f32 = jnp.float32
    nrm = lambda k, shape, scale: jax.random.normal(k, shape, f32) * scale
    return {
        'x': nrm(ks[0], (BATCH, SEQ, D_MODEL), 1.0),
        'rel_bias_table': nrm(ks[1], (N_BUCKETS, B_HEADS), 0.5),
        'mix_norm_g': 1.0 + nrm(ks[2], (DEPTH, D_MODEL), 0.02),
        'w_in': nrm(ks[3], (DEPTH, D_MODEL, PROJ_WIDTH), D_MODEL ** -0.5),
        'gate_norm_g': 1.0 + nrm(ks[4], (DEPTH, A_GROUPS, A_GROUP_DIM), 0.02),
        'gate_norm_b': nrm(ks[5], (DEPTH, A_GROUPS, A_GROUP_DIM), 0.02),
        'w_spatial': nrm(ks[6], (DEPTH, A_GROUPS, CHUNK, CHUNK), CHUNK ** -0.5),
        'b_spatial': 1.0 + nrm(ks[7], (DEPTH, A_GROUPS, CHUNK), 0.1),
        'attn_sinks': nrm(ks[8], (DEPTH, B_HEADS), 0.5),
        'out_norm_a_g': 1.0 + nrm(ks[9], (DEPTH, A_WIDTH), 0.02),
        'out_norm_b_g': 1.0 + nrm(ks[10], (DEPTH, B_WIDTH), 0.02),
        'w_out': nrm(ks[11], (DEPTH, MIX_WIDTH, D_MODEL), MIX_WIDTH ** -0.5),
        'ffn_norm_g': 1.0 + nrm(ks[12], (DEPTH, D_MODEL), 0.02),
        'w_up': nrm(ks[13], (DEPTH, D_MODEL, D_FF), D_MODEL ** -0.5),
        'w_down': nrm(ks[14], (DEPTH, D_FF, D_MODEL), D_FF ** -0.5),
        'final_norm_g': 1.0 + nrm(ks[15], (D_MODEL,), 0.02),
    }


def reference(x, rel_bias_table, mix_norm_g, w_in, gate_norm_g, gate_norm_b, w_spatial,
              b_spatial, attn_sinks, out_norm_a_g, out_norm_b_g, w_out, ffn_norm_g,
              w_up, w_down, final_norm_g):
    Bsz, S, _ = x.shape
    rel_bias = t5_relative_bias(rel_bias_table)
    h = x
    for layer in range(DEPTH):
        n = rms_norm(h, mix_norm_g[layer])
        proj = n @ w_in[layer]
        u, v, q, k, va = jnp.split(proj, SPLITS, axis=-1)
        a_out = spatial_gating(
            u.reshape(Bsz, S, A_GROUPS, A_GROUP_DIM),
            v.reshape(Bsz, S, A_GROUPS, A_GROUP_DIM),
            gate_norm_g[layer], gate_norm_b[layer], w_spatial[layer], b_spatial[layer],
        ).reshape(Bsz, S, A_WIDTH)
        b_out = sliding_window_attention(
            q.reshape(Bsz, S, B_HEADS, HEAD_DIM),
            k.reshape(Bsz, S, B_KV_HEADS, HEAD_DIM),
            va.reshape(Bsz, S, B_KV_HEADS, HEAD_DIM),
            attn_sinks[layer], rel_bias,
        )
        mixed = jnp.concatenate(
            [rms_norm(a_out, out_norm_a_g[layer]), rms_norm(b_out, out_norm_b_g[layer])], axis=-1)
        h = h + mixed @ w_out[layer]
        z = jax.nn.relu(rms_norm(h, ffn_norm_g[layer]) @ w_up[layer])
        h = h + (z * z) @ w_down[layer]
    return rms_norm(h, final_norm_g)
```

```python
import functools
import math

import jax
import jax.numpy as jnp
from jax import lax
from jax.experimental import pallas as pl
from jax.experimental.pallas import tpu as pltpu

D_MODEL = 2048
CHUNK = 128
A_GROUPS = 8
A_GROUP_DIM = 128
A_WIDTH = A_GROUPS * A_GROUP_DIM
HEAD_DIM = 64
B_HEADS = 16
B_KV_HEADS = 2
Q_PER_KV = B_HEADS // B_KV_HEADS
PAIRS_PER_KV = Q_PER_KV // 2
B_WIDTH = B_HEADS * HEAD_DIM
KV_WIDTH = B_KV_HEADS * HEAD_DIM
N_BUCKETS = 32
MAX_DISTANCE = 128
MIX_WIDTH = A_WIDTH + B_WIDTH
D_FF = 4 * D_MODEL
EPS = 1e-5
NEG = -1e30

V7X_VMEM_LIMIT_BYTES = 56 * 1024 * 1024

BF16 = jnp.bfloat16
F32 = jnp.float32


def _rms_scale(xf):
    return lax.rsqrt(jnp.mean(xf * xf, axis=-1, keepdims=True) + EPS)


def _params(n_axes):
    return pltpu.CompilerParams(
        dimension_semantics=("arbitrary",) * n_axes,
        vmem_limit_bytes=V7X_VMEM_LIMIT_BYTES)


def _const_spec(shape):
    return pl.BlockSpec(shape, lambda *_: (0,) * len(shape), pipeline_mode=pl.Buffered(1))


def _bias_kernel(table_ref, bucket_ref, o_ref):
    h = pl.program_id(0)
    bucket = bucket_ref[...]
    val = jnp.full(bucket.shape, NEG, F32)
    for b in range(N_BUCKETS):
        val = jnp.where(bucket == b, table_ref[b, h], val)
    col = lax.broadcasted_iota(jnp.int32, bucket.shape, 1)
    o_ref[0, 0] = val
    o_ref[1, 0] = jnp.where(col < CHUNK, NEG, val)


def _attention_bias(rel_bias_table, bucket):
    return pl.pallas_call(
        _bias_kernel,
        out_shape=jax.ShapeDtypeStruct(
            (2, B_KV_HEADS, PAIRS_PER_KV * CHUNK, 4 * CHUNK), F32),
        grid=(B_HEADS,),
        in_specs=[
            pl.BlockSpec(memory_space=pltpu.SMEM),
            pl.BlockSpec((CHUNK, 2 * CHUNK), lambda h: (0, 0)),
        ],
        out_specs=pl.BlockSpec(
            (2, 1, CHUNK, 2 * CHUNK),
            lambda h: (0, h // Q_PER_KV, (h % Q_PER_KV) // 2, h % 2)),
        compiler_params=_params(1),
        name="attn_bias",
    )(rel_bias_table, bucket)


def _in_proj_kernel(x_ref, g_ref, w_ref, u_ref, v_ref, q_ref, kv_ref):
    xf = x_ref[...]
    n = (xf * _rms_scale(xf) * g_ref[...]).astype(BF16)
    col = 0
    for o_ref in (u_ref, v_ref, q_ref, kv_ref):
        width = o_ref.shape[1]
        o_ref[...] = jnp.dot(n, w_ref[:, col:col + width],
                             preferred_element_type=F32).astype(BF16)
        col += width


def _in_proj(x2, g, w_in, tm):
    t = x2.shape[0]
    widths = (A_WIDTH, A_WIDTH, B_WIDTH, 2 * KV_WIDTH)
    return pl.pallas_call(
        _in_proj_kernel,
        out_shape=[jax.ShapeDtypeStruct((t, w), BF16) for w in widths],
        grid=(t // tm,),
        in_specs=[
            pl.BlockSpec((tm, D_MODEL), lambda i: (i, 0)),
            _const_spec((1, D_MODEL)),
            _const_spec(w_in.shape),
        ],
        out_specs=[pl.BlockSpec((tm, w), lambda i: (i, 0)) for w in widths],
        compiler_params=_params(1),
        name="in_proj",
    )(x2, g, w_in)


def _half_swapped_pair(t):
    lane = lax.broadcasted_iota(jnp.int32, t.shape, 1)
    lo = lane < HEAD_DIM
    tr = pltpu.roll(t, HEAD_DIM, axis=1)
    zero = jnp.zeros_like(t)
    kv0 = jnp.concatenate([jnp.where(lo, t, zero), jnp.where(lo, zero, tr)], axis=0)
    kv1 = jnp.concatenate([jnp.where(lo, tr, zero), jnp.where(lo, zero, t)], axis=0)
    return kv0, kv1


def _mixers_kernel(chunks_per_seq, u_ref, v_ref, q_ref, kv_ref, kvp_ref, lng_ref, lnb_ref,
                   ws_ref, bs_ref, sinks_ref, bias_ref, ga_ref, gb_ref, o_ref,
                   ws_bf, kvbuf, a_buf, b_buf):
    tq = u_ref.shape[0]
    n_chunks = tq // CHUNK
    step = pl.program_id(0)

    @pl.when(step == 0)
    def _():
        row = lax.broadcasted_iota(jnp.int32, (CHUNK, CHUNK), 0)
        col = lax.broadcasted_iota(jnp.int32, (CHUNK, CHUNK), 1)
        for g in range(A_GROUPS):
            ws_bf[g] = jnp.where(row >= col, ws_ref[g], 0.0).astype(BF16)

    kvbuf[0:CHUNK, :] = kvp_ref[...]
    kvbuf[CHUNK:CHUNK + tq, :] = kv_ref[...]

    def chunk_body(c, carry):
        r0 = pl.multiple_of(c * CHUNK, CHUNK)
        rows = pl.ds(r0, CHUNK)

        ss_a = jnp.zeros((CHUNK, A_GROUP_DIM), F32)
        for g in range(A_GROUPS):
            cols = slice(g * A_GROUP_DIM, (g + 1) * A_GROUP_DIM)
            ug = jax.nn.gelu(u_ref[rows, cols].astype(F32))
            vg = jax.nn.gelu(v_ref[rows, cols].astype(F32))
            mu = jnp.mean(vg, axis=-1, keepdims=True)
            vc = vg - mu
            var = jnp.mean(vc * vc, axis=-1, keepdims=True)
            vn = vc * lax.rsqrt(var + EPS) * lng_ref[:, cols] + lnb_ref[:, cols]
            mixed = jnp.dot(ws_bf[g], vn.astype(BF16), preferred_element_type=F32) + bs_ref[g]
            ag = ug * mixed
            a_buf[:, cols] = ag
            ss_a = ss_a + ag * ag
        scale_a = lax.rsqrt(jnp.sum(ss_a, axis=-1, keepdims=True) * (1.0 / A_WIDTH) + EPS)
        o_ref[rows, 0:A_WIDTH] = (a_buf[...] * scale_a * ga_ref[...]).astype(BF16)

        band = kvbuf[pl.ds(r0, 2 * CHUNK), :]
        k_stacks = _half_swapped_pair(band[:, :KV_WIDTH].astype(F32) * (HEAD_DIM ** -0.5))
        v_stacks = _half_swapped_pair(band[:, KV_WIDTH:].astype(F32))
        first = ((step * n_chunks + c) % chunks_per_seq == 0).astype(jnp.int32)
        lane = lax.broadcasted_iota(jnp.int32, (CHUNK, 2 * HEAD_DIM), 1)
        lo = lane < HEAD_DIM
        ss_b = jnp.zeros((CHUNK, 2 * HEAD_DIM), F32)
        for g in range(B_KV_HEADS):
            q_stack = jnp.concatenate(
                [q_ref[rows, (PAIRS_PER_KV * g + p) * 128:(PAIRS_PER_KV * g + p + 1) * 128]
                 for p in range(PAIRS_PER_KV)], axis=0)
            s_all = lax.dot_general(q_stack, k_stacks[g].astype(BF16),
                                    (((1,), (1,)), ((), ())),
                                    preferred_element_type=F32)
            p_rows = []
            rdens = []
            for p in range(PAIRS_PER_KV):
                p_cols = []
                rden_pair = []
                for par in range(2):
                    head = Q_PER_KV * g + 2 * p + par
                    rs = slice(p * CHUNK, (p + 1) * CHUNK)
                    cs = slice(par * 2 * CHUNK, (par + 1) * 2 * CHUNK)
                    s = s_all[rs, cs] + bias_ref[first, g, rs, cs]
                    sink = sinks_ref[head]
                    m = jnp.maximum(jnp.max(s, axis=-1, keepdims=True), sink)
                    e = jnp.exp(s - m)
                    den = jnp.sum(e, axis=-1, keepdims=True) + jnp.exp(sink - m)
                    p_cols.append(e.astype(BF16))
                    rden_pair.append(1.0 / den)
                p_rows.append(jnp.concatenate(p_cols, axis=1))
                rdens.append(rden_pair)
            probs = jnp.concatenate(p_rows, axis=0)
            o_all = jnp.dot(probs, v_stacks[g].astype(BF16), preferred_element_type=F32)
            for p in range(PAIRS_PER_KV):
                blk = PAIRS_PER_KV * g + p
                o = o_all[p * CHUNK:(p + 1) * CHUNK] * jnp.where(lo, rdens[p][0], rdens[p][1])
                b_buf[:, blk * 128:(blk + 1) * 128] = o
                ss_b = ss_b + o * o
        scale_b = lax.rsqrt(jnp.sum(ss_b, axis=-1, keepdims=True) * (1.0 / B_WIDTH) + EPS)
        o_ref[rows, A_WIDTH:MIX_WIDTH] = (b_buf[...] * scale_b * gb_ref[...]).astype(BF16)
        return carry

    lax.fori_loop(0, n_chunks, chunk_body, 0)


def _mixers(u, v, q, kv, lng, lnb, w_s, b_s, sinks, bias, ga, gb, tq, seq):
    t = u.shape[0]
    blocks_per_tile = tq // CHUNK
    return pl.pallas_call(
        functools.partial(_mixers_kernel, seq // CHUNK),
        out_shape=jax.ShapeDtypeStruct((t, MIX_WIDTH), BF16),
        grid=(t // tq,),
        in_specs=[
            pl.BlockSpec((tq, A_WIDTH), lambda i: (i, 0)),
            pl.BlockSpec((tq, A_WIDTH), lambda i: (i, 0)),
            pl.BlockSpec((tq, B_WIDTH), lambda i: (i, 0)),
            pl.BlockSpec((tq, 2 * KV_WIDTH), lambda i: (i, 0)),
            pl.BlockSpec((CHUNK, 2 * KV_WIDTH),
                         lambda i: (jnp.maximum(i * blocks_per_tile - 1, 0), 0)),
            _const_spec((1, A_WIDTH)),
            _const_spec((1, A_WIDTH)),
            _const_spec(w_s.shape),
            _const_spec(b_s.shape),
            pl.BlockSpec(memory_space=pltpu.SMEM),
            _const_spec(bias.shape),
            _const_spec((1, A_WIDTH)),
            _const_spec((1, B_WIDTH)),
        ],
        out_specs=pl.BlockSpec((tq, MIX_WIDTH), lambda i: (i, 0)),
        scratch_shapes=[
            pltpu.VMEM((A_GROUPS, CHUNK, CHUNK), BF16),
            pltpu.VMEM((tq + CHUNK, 2 * KV_WIDTH), BF16),
            pltpu.VMEM((CHUNK, A_WIDTH), F32),
            pltpu.VMEM((CHUNK, B_WIDTH), F32),
        ],
        compiler_params=_params(1),
        name="mixers",
    )(u, v, q, kv, kv, lng, lnb, w_s, b_s, sinks, bias, ga, gb)


def _out_proj_kernel(x_ref, m_ref, w_ref, o_ref):
    o_ref[...] = x_ref[...] + jnp.dot(m_ref[...], w_ref[...], preferred_element_type=F32)


def _out_proj(x2, mixed, w_out, tm):
    t = x2.shape[0]
    return pl.pallas_call(
        _out_proj_kernel,
        out_shape=jax.ShapeDtypeStruct((t, D_MODEL), F32),
        grid=(t // tm,),
        in_specs=[
            pl.BlockSpec((tm, D_MODEL), lambda i: (i, 0)),
            pl.BlockSpec((tm, MIX_WIDTH), lambda i: (i, 0)),
            _const_spec(w_out.shape),
        ],
        out_specs=pl.BlockSpec((tm, D_MODEL), lambda i: (i, 0)),
        compiler_params=_params(1),
        name="out_proj",
    )(x2, mixed, w_out)


def _ffn_kernel(final_norm, h_ref, g_ref, wu_ref, wd_ref, gf_ref, o_ref, n_ref):
    f = pl.program_id(1)

    @pl.when(f == 0)
    def _():
        hf = h_ref[...]
        n_ref[...] = (hf * _rms_scale(hf) * g_ref[...]).astype(BF16)
        o_ref[...] = hf

    z = jnp.maximum(jnp.dot(n_ref[...], wu_ref[...], preferred_element_type=F32), 0.0)
    o_ref[...] += jnp.dot((z * z).astype(BF16), wd_ref[...], preferred_element_type=F32)

    if final_norm:
        @pl.when(f == pl.num_programs(1) - 1)
        def _():
            y = o_ref[...]
            o_ref[...] = y * _rms_scale(y) * gf_ref[...]


def _ffn(h, g, w_up, w_down, gf, final_norm, tm, tf):
    t = h.shape[0]
    return pl.pallas_call(
        functools.partial(_ffn_kernel, final_norm),
        out_shape=jax.ShapeDtypeStruct((t, D_MODEL), F32),
        grid=(t // tm, D_FF // tf),
        in_specs=[
            pl.BlockSpec((tm, D_MODEL), lambda i, f: (i, 0)),
            _const_spec((1, D_MODEL)),
            pl.BlockSpec((D_MODEL, tf), lambda i, f: (0, f)),
            pl.BlockSpec((tf, D_MODEL), lambda i, f: (f, 0)),
            _const_spec((1, D_MODEL)),
        ],
        out_specs=pl.BlockSpec((tm, D_MODEL), lambda i, f: (i, 0)),
        scratch_shapes=[pltpu.VMEM((tm, D_MODEL), BF16)],
        compiler_params=_params(2),
        name="ffn",
    )(h, g, w_up, w_down, gf)


def _relative_buckets():
    i = jnp.arange(CHUNK)[:, None]
    j = jnp.arange(2 * CHUNK)[None, :]
    rel = i + CHUNK - j
    relc = jnp.maximum(rel, 0)
    n_exact = N_BUCKETS // 2
    relf = jnp.maximum(relc, n_exact).astype(F32)
    large = n_exact + (jnp.log(relf / n_exact) / math.log(MAX_DISTANCE / n_exact)
                       * (N_BUCKETS - n_exact)).astype(jnp.int32)
    large = jnp.minimum(large, N_BUCKETS - 1)
    bucket = jnp.where(relc < n_exact, relc, large)
    return jnp.where((rel >= 0) & (rel < CHUNK), bucket, -1).astype(jnp.int32)


def kernel(x, rel_bias_table, mix_norm_g, w_in, gate_norm_g, gate_norm_b, w_spatial, b_spatial,
           attn_sinks, out_norm_a_g, out_norm_b_g, w_out, ffn_norm_g, w_up, w_down, final_norm_g):
    bsz, seq, d = x.shape
    depth = w_in.shape[0]
    t = bsz * seq
    bias = _attention_bias(rel_bias_table.astype(F32), _relative_buckets())
    h = x.reshape(t, d)
    for layer in range(depth):
        u, v, q, kv = _in_proj(h, mix_norm_g[layer].reshape(1, d), w_in[layer].astype(BF16), tm=512)
        b_s = jnp.broadcast_to(b_spatial[layer][:, :, None], (A_GROUPS, CHUNK, A_GROUP_DIM))
        mixed = _mixers(
            u, v, q, kv,
            gate_norm_g[layer].reshape(1, A_WIDTH), gate_norm_b[layer].reshape(1, A_WIDTH),
            w_spatial[layer], b_s, attn_sinks[layer], bias,
            out_norm_a_g[layer].reshape(1, A_WIDTH), out_norm_b_g[layer].reshape(1, B_WIDTH),
            tq=512, seq=seq)
        h_mid = _out_proj(h, mixed, w_out[layer].astype(BF16), tm=512)
        h = _ffn(h_mid, ffn_norm_g[layer].reshape(1, d), w_up[layer].astype(BF16),
                 w_down[layer].astype(BF16), final_norm_g.reshape(1, d),
                 final_norm=layer == depth - 1, tm=1024, tf=512)
    return h.reshape(bsz, seq, d)
```

```python
import functools
import math

import jax
import jax.numpy as jnp
from jax import lax
from jax.experimental import pallas as pl
from jax.experimental.pallas import tpu as pltpu

D_MODEL = 2048
CHUNK = 128
A_GROUPS = 8
A_GROUP_DIM = 128
A_WIDTH = A_GROUPS * A_GROUP_DIM
HEAD_DIM = 64
B_HEADS = 16
B_KV_HEADS = 2
Q_PER_KV = B_HEADS // B_KV_HEADS
PAIRS_PER_KV = Q_PER_KV // 2
B_WIDTH = B_HEADS * HEAD_DIM
KV_WIDTH = B_KV_HEADS * HEAD_DIM
N_BUCKETS = 32
MAX_DISTANCE = 128
MIX_WIDTH = A_WIDTH + B_WIDTH
D_FF = 4 * D_MODEL
EPS = 1e-5
NEG = -1e30

V7X_VMEM_LIMIT_BYTES = 56 * 1024 * 1024
MXU_COLS = 256
MIX_PIECES_PER_CHUNK = A_GROUPS + 1 + B_KV_HEADS * (PAIRS_PER_KV + 2) + 1

BF16 = jnp.bfloat16
F32 = jnp.float32


def _rms_scale(xf):
    return lax.rsqrt(jnp.mean(xf * xf, axis=-1, keepdims=True) + EPS)


def _params(n_axes):
    return pltpu.CompilerParams(
        dimension_semantics=("arbitrary",) * n_axes,
        vmem_limit_bytes=V7X_VMEM_LIMIT_BYTES)


def _const_spec(shape):
    return pl.BlockSpec(shape, lambda *_: (0,) * len(shape), pipeline_mode=pl.Buffered(1))


def _bias_kernel(table_ref, bucket_ref, o_ref):
    h = pl.program_id(0)
    bucket = bucket_ref[...]
    val = jnp.full(bucket.shape, NEG, F32)
    for b in range(N_BUCKETS):
        val = jnp.where(bucket == b, table_ref[b, h], val)
    col = lax.broadcasted_iota(jnp.int32, bucket.shape, 1)
    o_ref[0, 0] = val
    o_ref[1, 0] = jnp.where(col < CHUNK, NEG, val)


def _attention_bias(rel_bias_table, bucket):
    return pl.pallas_call(
        _bias_kernel,
        out_shape=jax.ShapeDtypeStruct(
            (2, B_KV_HEADS, PAIRS_PER_KV * CHUNK, 4 * CHUNK), F32),
        grid=(B_HEADS,),
        in_specs=[
            pl.BlockSpec(memory_space=pltpu.SMEM),
            pl.BlockSpec((CHUNK, 2 * CHUNK), lambda h: (0, 0)),
        ],
        out_specs=pl.BlockSpec(
            (2, 1, CHUNK, 2 * CHUNK),
            lambda h: (0, h // Q_PER_KV, (h % Q_PER_KV) // 2, h % 2)),
        compiler_params=_params(1),
        name="attn_bias",
    )(rel_bias_table, bucket)


def _half_swapped_pair(t):
    lane = lax.broadcasted_iota(jnp.int32, t.shape, 1)
    lo = lane < HEAD_DIM
    tr = pltpu.roll(t, HEAD_DIM, axis=1)
    zero = jnp.zeros_like(t)
    kv0 = jnp.concatenate([jnp.where(lo, t, zero), jnp.where(lo, zero, tr)], axis=0)
    kv1 = jnp.concatenate([jnp.where(lo, tr, zero), jnp.where(lo, zero, t)], axis=0)
    return kv0, kv1


def _mix_block_kernel(n_tiles, chunks_per_seq,
                      x1_ref, x3_ref, g_ref, win_ref, lng_ref, lnb_ref, ws_ref, bs_ref,
                      sinks_ref, bias_ref, ga_ref, gb_ref, wout_ref,
                      o_ref,
                      ws_bf, n_buf, u_nx, v_nx, q_nx, kv_nx, u_cu, v_cu, q_cu, kvbuf,
                      mix_nx, mix_cu, a_buf, b_buf):
    tq = x1_ref.shape[0]
    n_chunks = tq // CHUNK
    step = pl.program_id(0)

    def project_pieces():
        xf = x1_ref[...]
        n_buf[...] = (xf * _rms_scale(xf) * g_ref[...]).astype(BF16)
        col = 0
        for dst in (u_nx, v_nx, q_nx, kv_nx):
            for c0 in range(0, dst.shape[1], MXU_COLS):
                def piece(dst=dst, c0=c0, col=col):
                    dst[:, c0:c0 + MXU_COLS] = jnp.dot(
                        n_buf[...], win_ref[:, col + c0:col + c0 + MXU_COLS],
                        preferred_element_type=F32).astype(BF16)
                yield piece
            col += dst.shape[1]

    def take_projection():
        kvbuf[0:CHUNK, :] = kvbuf[tq:tq + CHUNK, :]
        kvbuf[CHUNK:CHUNK + tq, :] = kv_nx[...]
        u_cu[...] = u_nx[...]
        v_cu[...] = v_nx[...]
        q_cu[...] = q_nx[...]

    def mix_a(c):
        rows = slice(c * CHUNK, (c + 1) * CHUNK)
        ss_a = jnp.zeros((CHUNK, A_GROUP_DIM), F32)
        for g in range(A_GROUPS):
            cols = slice(g * A_GROUP_DIM, (g + 1) * A_GROUP_DIM)
            ug = jax.nn.gelu(u_cu[rows, cols].astype(F32))
            vg = jax.nn.gelu(v_cu[rows, cols].astype(F32))
            mu = jnp.mean(vg, axis=-1, keepdims=True)
            vc = vg - mu
            var = jnp.mean(vc * vc, axis=-1, keepdims=True)
            vn = vc * lax.rsqrt(var + EPS) * lng_ref[:, cols] + lnb_ref[:, cols]
            mixed = jnp.dot(ws_bf[g], vn.astype(BF16), preferred_element_type=F32) + bs_ref[g]
            ag = ug * mixed
            a_buf[c, :, cols] = ag
            ss_a = ss_a + ag * ag
            yield
        scale_a = lax.rsqrt(jnp.sum(ss_a, axis=-1, keepdims=True) * (1.0 / A_WIDTH) + EPS)
        mix_nx[rows, 0:A_WIDTH] = (a_buf[c] * scale_a * ga_ref[...]).astype(BF16)
        yield

    def mix_b(c):
        rows = slice(c * CHUNK, (c + 1) * CHUNK)
        band = kvbuf[c * CHUNK:(c + 2) * CHUNK, :]
        k_stacks = _half_swapped_pair(band[:, :KV_WIDTH].astype(F32) * (HEAD_DIM ** -0.5))
        v_stacks = _half_swapped_pair(band[:, KV_WIDTH:].astype(F32))
        chunk_id = (step - 1) * n_chunks + c
        first = (chunk_id % chunks_per_seq == 0).astype(jnp.int32)
        lane = lax.broadcasted_iota(jnp.int32, (CHUNK, 2 * HEAD_DIM), 1)
        lo = lane < HEAD_DIM
        ss_b = jnp.zeros((CHUNK, 2 * HEAD_DIM), F32)
        for g in range(B_KV_HEADS):
            q_stack = jnp.concatenate(
                [q_cu[rows, (PAIRS_PER_KV * g + p) * 128:(PAIRS_PER_KV * g + p + 1) * 128]
                 for p in range(PAIRS_PER_KV)], axis=0)
            s_all = lax.dot_general(q_stack, k_stacks[g].astype(BF16),
                                    (((1,), (1,)), ((), ())),
                                    preferred_element_type=F32)
            yield
            p_rows = []
            rdens = []
            for p in range(PAIRS_PER_KV):
                p_cols = []
                rden_pair = []
                for par in range(2):
                    head = Q_PER_KV * g + 2 * p + par
                    rs = slice(p * CHUNK, (p + 1) * CHUNK)
                    cs = slice(par * 2 * CHUNK, (par + 1) * 2 * CHUNK)
                    s = s_all[rs, cs] + bias_ref[first, g, rs, cs]
                    sink = sinks_ref[head]
                    m = jnp.maximum(jnp.max(s, axis=-1, keepdims=True), sink)
                    e = jnp.exp(s - m)
                    den = jnp.sum(e, axis=-1, keepdims=True) + jnp.exp(sink - m)
                    p_cols.append(e.astype(BF16))
                    rden_pair.append(1.0 / den)
                p_rows.append(jnp.concatenate(p_cols, axis=1))
                rdens.append(rden_pair)
                yield
            probs = jnp.concatenate(p_rows, axis=0)
            o_all = jnp.dot(probs, v_stacks[g].astype(BF16), preferred_element_type=F32)
            for p in range(PAIRS_PER_KV):
                blk = PAIRS_PER_KV * g + p
                o = o_all[p * CHUNK:(p + 1) * CHUNK] * jnp.where(lo, rdens[p][0], rdens[p][1])
                b_buf[c, :, blk * 128:(blk + 1) * 128] = o
                ss_b = ss_b + o * o
            yield
        scale_b = lax.rsqrt(jnp.sum(ss_b, axis=-1, keepdims=True) * (1.0 / B_WIDTH) + EPS)
        mix_nx[rows, A_WIDTH:MIX_WIDTH] = (b_buf[c] * scale_b * gb_ref[...]).astype(BF16)
        yield

    def mix_pieces():
        for c in range(n_chunks):
            gens = [mix_b(c), mix_a(c)]
            while gens:
                for gen in list(gens):
                    try:
                        next(gen)
                        yield
                    except StopIteration:
                        gens.remove(gen)

    def output_pieces():
        mix_cu[...] = mix_nx[...]
        for c0 in range(0, D_MODEL, MXU_COLS):
            def piece(c0=c0):
                o_ref[:, c0:c0 + MXU_COLS] = x3_ref[:, c0:c0 + MXU_COLS] + jnp.dot(
                    mix_cu[...], wout_ref[:, c0:c0 + MXU_COLS], preferred_element_type=F32)
            yield piece

    def run(stage1, stage2, stage3):
        matmuls = []
        if stage3:
            matmuls += list(output_pieces())
        if stage2:
            take_projection()
        if stage1:
            matmuls += list(project_pieces())
        if not stage2:
            for piece in matmuls:
                piece()
            return
        n_yields = n_chunks * MIX_PIECES_PER_CHUNK
        matmuls[0]()
        done = 1
        for k, _ in enumerate(mix_pieces(), start=1):
            want = 1 + (k * (len(matmuls) - 1)) // n_yields
            while done < want:
                matmuls[done]()
                done += 1
        assert done == len(matmuls), (done, len(matmuls))

    @pl.when(step == 0)
    def _():
        row = lax.broadcasted_iota(jnp.int32, (CHUNK, CHUNK), 0)
        col = lax.broadcasted_iota(jnp.int32, (CHUNK, CHUNK), 1)
        for g in range(A_GROUPS):
            ws_bf[g] = jnp.where(row >= col, ws_ref[g], 0.0).astype(BF16)
        kvbuf[...] = jnp.zeros_like(kvbuf)
        run(True, False, False)

    @pl.when(step == 1)
    def _():
        run(True, True, False)

    @pl.when((step >= 2) & (step < n_tiles))
    def _():
        run(True, True, True)

    @pl.when(step == n_tiles)
    def _():
        run(False, True, True)

    @pl.when(step == n_tiles + 1)
    def _():
        run(False, False, True)


def _mix_block(x2, g, w_in, lng, lnb, w_s, b_s, sinks, bias, ga, gb, w_out, tq, seq):
    t = x2.shape[0]
    n_tiles = t // tq
    n_chunks = tq // CHUNK
    proj = lambda w: pltpu.VMEM((tq, w), BF16)
    return pl.pallas_call(
        functools.partial(_mix_block_kernel, n_tiles, seq // CHUNK),
        out_shape=jax.ShapeDtypeStruct((t, D_MODEL), F32),
        grid=(n_tiles + 2,),
        in_specs=[
            pl.BlockSpec((tq, D_MODEL), lambda s: (jnp.minimum(s, n_tiles - 1), 0)),
            pl.BlockSpec((tq, D_MODEL), lambda s: (jnp.clip(s - 2, 0, n_tiles - 1), 0)),
            _const_spec((1, D_MODEL)),
            _const_spec(w_in.shape),
            _const_spec((1, A_WIDTH)),
            _const_spec((1, A_WIDTH)),
            _const_spec(w_s.shape),
            _const_spec(b_s.shape),
            pl.BlockSpec(memory_space=pltpu.SMEM),
            _const_spec(bias.shape),
            _const_spec((1, A_WIDTH)),
            _const_spec((1, B_WIDTH)),
            _const_spec(w_out.shape),
        ],
        out_specs=pl.BlockSpec((tq, D_MODEL), lambda s: (jnp.clip(s - 2, 0, n_tiles - 1), 0)),
        scratch_shapes=[
            pltpu.VMEM((A_GROUPS, CHUNK, CHUNK), BF16),
            proj(D_MODEL),
            proj(A_WIDTH), proj(A_WIDTH), proj(B_WIDTH), proj(2 * KV_WIDTH),
            proj(A_WIDTH), proj(A_WIDTH), proj(B_WIDTH),
            pltpu.VMEM((tq + CHUNK, 2 * KV_WIDTH), BF16),
            proj(MIX_WIDTH), proj(MIX_WIDTH),
            pltpu.VMEM((n_chunks, CHUNK, A_WIDTH), F32),
            pltpu.VMEM((n_chunks, CHUNK, B_WIDTH), F32),
        ],
        compiler_params=_params(1),
        name="mix_block",
    )(x2, x2, g, w_in, lng, lnb, w_s, b_s, sinks, bias, ga, gb, w_out)


def _ffn_kernel(final_norm, h_ref, g_ref, wu_ref, wd_ref, gf_ref, o_ref, n_ref):
    f = pl.program_id(1)

    @pl.when(f == 0)
    def _():
        hf = h_ref[...]
        n_ref[...] = (hf * _rms_scale(hf) * g_ref[...]).astype(BF16)
        o_ref[...] = hf

    z = jnp.maximum(jnp.dot(n_ref[...], wu_ref[...], preferred_element_type=F32), 0.0)
    o_ref[...] += jnp.dot((z * z).astype(BF16), wd_ref[...], preferred_element_type=F32)

    if final_norm:
        @pl.when(f == pl.num_programs(1) - 1)
        def _():
            y = o_ref[...]
            o_ref[...] = y * _rms_scale(y) * gf_ref[...]


def _ffn(h, g, w_up, w_down, gf, final_norm, tm, tf):
    t = h.shape[0]
    return pl.pallas_call(
        functools.partial(_ffn_kernel, final_norm),
        out_shape=jax.ShapeDtypeStruct((t, D_MODEL), F32),
        grid=(t // tm, D_FF // tf),
        in_specs=[
            pl.BlockSpec((tm, D_MODEL), lambda i, f: (i, 0)),
            _const_spec((1, D_MODEL)),
            pl.BlockSpec((D_MODEL, tf), lambda i, f: (0, f)),
            pl.BlockSpec((tf, D_MODEL), lambda i, f: (f, 0)),
            _const_spec((1, D_MODEL)),
        ],
        out_specs=pl.BlockSpec((tm, D_MODEL), lambda i, f: (i, 0)),
        scratch_shapes=[pltpu.VMEM((tm, D_MODEL), BF16)],
        compiler_params=_params(2),
        name="ffn",
    )(h, g, w_up, w_down, gf)


def _relative_buckets():
    i = jnp.arange(CHUNK)[:, None]
    j = jnp.arange(2 * CHUNK)[None, :]
    rel = i + CHUNK - j
    relc = jnp.maximum(rel, 0)
    n_exact = N_BUCKETS // 2
    relf = jnp.maximum(relc, n_exact).astype(F32)
    large = n_exact + (jnp.log(relf / n_exact) / math.log(MAX_DISTANCE / n_exact)
                       * (N_BUCKETS - n_exact)).astype(jnp.int32)
    large = jnp.minimum(large, N_BUCKETS - 1)
    bucket = jnp.where(relc < n_exact, relc, large)
    return jnp.where((rel >= 0) & (rel < CHUNK), bucket, -1).astype(jnp.int32)


def kernel(x, rel_bias_table, mix_norm_g, w_in, gate_norm_g, gate_norm_b, w_spatial, b_spatial,
           attn_sinks, out_norm_a_g, out_norm_b_g, w_out, ffn_norm_g, w_up, w_down, final_norm_g):
    bsz, seq, d = x.shape
    depth = w_in.shape[0]
    t = bsz * seq
    bias = _attention_bias(rel_bias_table.astype(F32), _relative_buckets())
    h = x.reshape(t, d)
    for layer in range(depth):
        b_s = jnp.broadcast_to(b_spatial[layer][:, :, None], (A_GROUPS, CHUNK, A_GROUP_DIM))
        h_mid = _mix_block(
            h, mix_norm_g[layer].reshape(1, d), w_in[layer].astype(BF16),
            gate_norm_g[layer].reshape(1, A_WIDTH), gate_norm_b[layer].reshape(1, A_WIDTH),
            w_spatial[layer], b_s, attn_sinks[layer], bias,
            out_norm_a_g[layer].reshape(1, A_WIDTH), out_norm_b_g[layer].reshape(1, B_WIDTH),
            w_out[layer].astype(BF16), tq=256, seq=seq)
        h = _ffn(h_mid, ffn_norm_g[layer].reshape(1, d), w_up[layer].astype(BF16),
                 w_down[layer].astype(BF16), final_norm_g.reshape(1, d),
                 final_norm=layer == depth - 1, tm=1024, tf=512)
    return h.reshape(bsz, seq, d)
```

```python
import functools
import math

import jax
import jax.numpy as jnp
from jax import lax
from jax.experimental import pallas as pl
from jax.experimental.pallas import tpu as pltpu

D_MODEL = 2048
CHUNK = 128
A_GROUPS = 8
A_GROUP_DIM = 128
A_WIDTH = A_GROUPS * A_GROUP_DIM
HEAD_DIM = 64
B_HEADS = 16
B_KV_HEADS = 2
Q_PER_KV = B_HEADS // B_KV_HEADS
PAIRS_PER_KV = Q_PER_KV // 2
B_WIDTH = B_HEADS * HEAD_DIM
KV_WIDTH = B_KV_HEADS * HEAD_DIM
N_BUCKETS = 32
MAX_DISTANCE = 128
MIX_WIDTH = A_WIDTH + B_WIDTH
D_FF = 4 * D_MODEL
EPS = 1e-5
NEG = -1e30

V7X_VMEM_LIMIT_BYTES = 56 * 1024 * 1024
MXU_COLS = 256
MIX_PIECES_PER_CHUNK = A_GROUPS + 1 + B_KV_HEADS * (PAIRS_PER_KV + 2) + 1

BF16 = jnp.bfloat16
F32 = jnp.float32


def _rms_scale(xf):
    return lax.rsqrt(jnp.mean(xf * xf, axis=-1, keepdims=True) + EPS)


def _params(n_axes):
    return pltpu.CompilerParams(
        dimension_semantics=("arbitrary",) * n_axes,
        vmem_limit_bytes=V7X_VMEM_LIMIT_BYTES)


def _const_spec(shape):
    return pl.BlockSpec(shape, lambda *_: (0,) * len(shape), pipeline_mode=pl.Buffered(1))


def _bias_kernel(table_ref, bucket_ref, o_ref):
    h = pl.program_id(0)
    bucket = bucket_ref[...]
    val = jnp.full(bucket.shape, NEG, F32)
    for b in range(N_BUCKETS):
        val = jnp.where(bucket == b, table_ref[b, h], val)
    col = lax.broadcasted_iota(jnp.int32, bucket.shape, 1)
    o_ref[0, 0] = val
    o_ref[1, 0] = jnp.where(col < CHUNK, NEG, val)


def _attention_bias(rel_bias_table, bucket):
    return pl.pallas_call(
        _bias_kernel,
        out_shape=jax.ShapeDtypeStruct(
            (2, B_KV_HEADS, PAIRS_PER_KV * CHUNK, 4 * CHUNK), F32),
        grid=(B_HEADS,),
        in_specs=[
            pl.BlockSpec(memory_space=pltpu.SMEM),
            pl.BlockSpec((CHUNK, 2 * CHUNK), lambda h: (0, 0)),
        ],
        out_specs=pl.BlockSpec(
            (2, 1, CHUNK, 2 * CHUNK),
            lambda h: (0, h // Q_PER_KV, (h % Q_PER_KV) // 2, h % 2)),
        compiler_params=_params(1),
        name="attn_bias",
    )(rel_bias_table, bucket)


def _half_swapped_pair(t):
    lane = lax.broadcasted_iota(jnp.int32, t.shape, 1)
    lo = lane < HEAD_DIM
    tr = pltpu.roll(t, HEAD_DIM, axis=1)
    zero = jnp.zeros_like(t)
    kv0 = jnp.concatenate([jnp.where(lo, t, zero), jnp.where(lo, zero, tr)], axis=0)
    kv1 = jnp.concatenate([jnp.where(lo, tr, zero), jnp.where(lo, zero, t)], axis=0)
    return kv0, kv1


def _mix_block_kernel(n_tiles, chunks_per_seq,
                      x1_ref, x3_ref, g_ref, win_ref, lng_ref, lnb_ref, ws_ref, bs_ref,
                      sinks_ref, bias_ref, ga_ref, gb_ref, wout_ref,
                      o_ref,
                      ws_bf, n_buf, u_nx, v_nx, q_nx, kv_nx, u_cu, v_cu, q_cu, kvbuf,
                      mix_nx, mix_cu, a_buf, b_buf):
    tq = x1_ref.shape[0]
    n_chunks = tq // CHUNK
    step = pl.program_id(0)

    def project_pieces():
        xf = x1_ref[...]
        n_buf[...] = (xf * _rms_scale(xf) * g_ref[...]).astype(BF16)
        col = 0
        for dst in (u_nx, v_nx, q_nx, kv_nx):
            for c0 in range(0, dst.shape[1], MXU_COLS):
                def piece(dst=dst, c0=c0, col=col):
                    dst[:, c0:c0 + MXU_COLS] = jnp.dot(
                        n_buf[...], win_ref[:, col + c0:col + c0 + MXU_COLS],
                        preferred_element_type=F32).astype(BF16)
                yield piece
            col += dst.shape[1]

    def take_projection():
        kvbuf[0:CHUNK, :] = kvbuf[tq:tq + CHUNK, :]
        kvbuf[CHUNK:CHUNK + tq, :] = kv_nx[...]
        u_cu[...] = u_nx[...]
        v_cu[...] = v_nx[...]
        q_cu[...] = q_nx[...]

    def mix_a(c):
        rows = slice(c * CHUNK, (c + 1) * CHUNK)
        ss_a = jnp.zeros((CHUNK, A_GROUP_DIM), F32)
        for g in range(A_GROUPS):
            cols = slice(g * A_GROUP_DIM, (g + 1) * A_GROUP_DIM)
            ug = jax.nn.gelu(u_cu[rows, cols].astype(F32))
            vg = jax.nn.gelu(v_cu[rows, cols].astype(F32))
            mu = jnp.mean(vg, axis=-1, keepdims=True)
            vc = vg - mu
            var = jnp.mean(vc * vc, axis=-1, keepdims=True)
            vn = vc * lax.rsqrt(var + EPS) * lng_ref[:, cols] + lnb_ref[:, cols]
            mixed = jnp.dot(ws_bf[g], vn.astype(BF16), preferred_element_type=F32) + bs_ref[g]
            ag = ug * mixed
            a_buf[c, :, cols] = ag
            ss_a = ss_a + ag * ag
            yield
        scale_a = lax.rsqrt(jnp.sum(ss_a, axis=-1, keepdims=True) * (1.0 / A_WIDTH) + EPS)
        mix_nx[rows, 0:A_WIDTH] = (a_buf[c] * scale_a * ga_ref[...]).astype(BF16)
        yield

    def mix_b(c):
        rows = slice(c * CHUNK, (c + 1) * CHUNK)
        band = kvbuf[c * CHUNK:(c + 2) * CHUNK, :]
        k_stacks = _half_swapped_pair(band[:, :KV_WIDTH].astype(F32) * (HEAD_DIM ** -0.5))
        v_stacks = _half_swapped_pair(band[:, KV_WIDTH:].astype(F32))
        chunk_id = (step - 1) * n_chunks + c
        first = (chunk_id % chunks_per_seq == 0).astype(jnp.int32)
        lane = lax.broadcasted_iota(jnp.int32, (CHUNK, 2 * HEAD_DIM), 1)
        lo = lane < HEAD_DIM
        ss_b = jnp.zeros((CHUNK, 2 * HEAD_DIM), F32)
        for g in range(B_KV_HEADS):
            q_stack = jnp.concatenate(
                [q_cu[rows, (PAIRS_PER_KV * g + p) * 128:(PAIRS_PER_KV * g + p + 1) * 128]
                 for p in range(PAIRS_PER_KV)], axis=0)
            s_all = lax.dot_general(q_stack, k_stacks[g].astype(BF16),
                                    (((1,), (1,)), ((), ())),
                                    preferred_element_type=F32)
            yield
            p_rows = []
            rdens = []
            for p in range(PAIRS_PER_KV):
                p_cols = []
                rden_pair = []
                for par in range(2):
                    head = Q_PER_KV * g + 2 * p + par
                    rs = slice(p * CHUNK, (p + 1) * CHUNK)
                    cs = slice(par * 2 * CHUNK, (par + 1) * 2 * CHUNK)
                    s = s_all[rs, cs] + bias_ref[first, g, rs, cs]
                    sink = sinks_ref[head]
                    m = jnp.maximum(jnp.max(s, axis=-1, keepdims=True), sink)
                    e = jnp.exp(s - m)
                    den = jnp.sum(e, axis=-1, keepdims=True) + jnp.exp(sink - m)
                    p_cols.append(e.astype(BF16))
                    rden_pair.append(1.0 / den)
                p_rows.append(jnp.concatenate(p_cols, axis=1))
                rdens.append(rden_pair)
                yield
            probs = jnp.concatenate(p_rows, axis=0)
            o_all = jnp.dot(probs, v_stacks[g].astype(BF16), preferred_element_type=F32)
            for p in range(PAIRS_PER_KV):
                blk = PAIRS_PER_KV * g + p
                o = o_all[p * CHUNK:(p + 1) * CHUNK] * jnp.where(lo, rdens[p][0], rdens[p][1])
                b_buf[c, :, blk * 128:(blk + 1) * 128] = o
                ss_b = ss_b + o * o
            yield
        scale_b = lax.rsqrt(jnp.sum(ss_b, axis=-1, keepdims=True) * (1.0 / B_WIDTH) + EPS)
        mix_nx[rows, A_WIDTH:MIX_WIDTH] = (b_buf[c] * scale_b * gb_ref[...]).astype(BF16)
        yield

    def mix_pieces():
        for c in range(n_chunks):
            gens = [mix_b(c), mix_a(c)]
            while gens:
                for gen in list(gens):
                    try:
                        next(gen)
                        yield
                    except StopIteration:
                        gens.remove(gen)

    def output_pieces():
        mix_cu[...] = mix_nx[...]
        for c0 in range(0, D_MODEL, MXU_COLS):
            def piece(c0=c0):
                o_ref[:, c0:c0 + MXU_COLS] = x3_ref[:, c0:c0 + MXU_COLS] + jnp.dot(
                    mix_cu[...], wout_ref[:, c0:c0 + MXU_COLS], preferred_element_type=F32)
            yield piece

    def run(stage1, stage2, stage3):
        matmuls = []
        if stage3:
            matmuls += list(output_pieces())
        if stage2:
            take_projection()
        if stage1:
            matmuls += list(project_pieces())
        if not stage2:
            for piece in matmuls:
                piece()
            return
        n_yields = n_chunks * MIX_PIECES_PER_CHUNK
        matmuls[0]()
        done = 1
        for k, _ in enumerate(mix_pieces(), start=1):
            want = 1 + (k * (len(matmuls) - 1)) // n_yields
            while done < want:
                matmuls[done]()
                done += 1
        assert done == len(matmuls), (done, len(matmuls))

    @pl.when(step == 0)
    def _():
        row = lax.broadcasted_iota(jnp.int32, (CHUNK, CHUNK), 0)
        col = lax.broadcasted_iota(jnp.int32, (CHUNK, CHUNK), 1)
        for g in range(A_GROUPS):
            ws_bf[g] = jnp.where(row >= col, ws_ref[g], 0.0).astype(BF16)
        kvbuf[...] = jnp.zeros_like(kvbuf)
        run(True, False, False)

    @pl.when(step == 1)
    def _():
        run(True, True, False)

    @pl.when((step >= 2) & (step < n_tiles))
    def _():
        run(True, True, True)

    @pl.when(step == n_tiles)
    def _():
        run(False, True, True)

    @pl.when(step == n_tiles + 1)
    def _():
        run(False, False, True)


def _mix_block(x2, g, w_in, lng, lnb, w_s, b_s, sinks, bias, ga, gb, w_out, tq, seq):
    t = x2.shape[0]
    n_tiles = t // tq
    n_chunks = tq // CHUNK
    proj = lambda w: pltpu.VMEM((tq, w), BF16)
    return pl.pallas_call(
        functools.partial(_mix_block_kernel, n_tiles, seq // CHUNK),
        out_shape=jax.ShapeDtypeStruct((t, D_MODEL), F32),
        grid=(n_tiles + 2,),
        in_specs=[
            pl.BlockSpec((tq, D_MODEL), lambda s: (jnp.minimum(s, n_tiles - 1), 0)),
            pl.BlockSpec((tq, D_MODEL), lambda s: (jnp.clip(s - 2, 0, n_tiles - 1), 0)),
            _const_spec((1, D_MODEL)),
            _const_spec(w_in.shape),
            _const_spec((1, A_WIDTH)),
            _const_spec((1, A_WIDTH)),
            _const_spec(w_s.shape),
            _const_spec(b_s.shape),
            pl.BlockSpec(memory_space=pltpu.SMEM),
            _const_spec(bias.shape),
            _const_spec((1, A_WIDTH)),
            _const_spec((1, B_WIDTH)),
            _const_spec(w_out.shape),
        ],
        out_specs=pl.BlockSpec((tq, D_MODEL), lambda s: (jnp.clip(s - 2, 0, n_tiles - 1), 0)),
        scratch_shapes=[
            pltpu.VMEM((A_GROUPS, CHUNK, CHUNK), BF16),
            proj(D_MODEL),
            proj(A_WIDTH), proj(A_WIDTH), proj(B_WIDTH), proj(2 * KV_WIDTH),
            proj(A_WIDTH), proj(A_WIDTH), proj(B_WIDTH),
            pltpu.VMEM((tq + CHUNK, 2 * KV_WIDTH), BF16),
            proj(MIX_WIDTH), proj(MIX_WIDTH),
            pltpu.VMEM((n_chunks, CHUNK, A_WIDTH), F32),
            pltpu.VMEM((n_chunks, CHUNK, B_WIDTH), F32),
        ],
        compiler_params=_params(1),
        name="mix_block",
    )(x2, x2, g, w_in, lng, lnb, w_s, b_s, sinks, bias, ga, gb, w_out)


def _ffn_kernel(final_norm, h_ref, g_ref, wu_ref, wd_ref, gf_ref, o_ref, n_ref):
    f = pl.program_id(1)

    @pl.when(f == 0)
    def _():
        hf = h_ref[...]
        n_ref[...] = (hf * _rms_scale(hf) * g_ref[...]).astype(BF16)
        o_ref[...] = hf

    z = jnp.maximum(jnp.dot(n_ref[...], wu_ref[...].astype(BF16),
                            preferred_element_type=F32), 0.0)
    o_ref[...] += jnp.dot((z * z).astype(BF16), wd_ref[...].astype(BF16),
                          preferred_element_type=F32)

    if final_norm:
        @pl.when(f == pl.num_programs(1) - 1)
        def _():
            y = o_ref[...]
            o_ref[...] = y * _rms_scale(y) * gf_ref[...]


def _ffn(h, g, w_up, w_down, gf, layer, final_norm, tm, tf):
    t = h.shape[0]
    return pl.pallas_call(
        functools.partial(_ffn_kernel, final_norm),
        out_shape=jax.ShapeDtypeStruct((t, D_MODEL), F32),
        grid=(t // tm, D_FF // tf),
        in_specs=[
            pl.BlockSpec((tm, D_MODEL), lambda i, f: (i, 0)),
            _const_spec((1, D_MODEL)),
            pl.BlockSpec((None, D_MODEL, tf), lambda i, f: (layer, 0, f)),
            pl.BlockSpec((None, tf, D_MODEL), lambda i, f: (layer, f, 0)),
            _const_spec((1, D_MODEL)),
        ],
        out_specs=pl.BlockSpec((tm, D_MODEL), lambda i, f: (i, 0)),
        scratch_shapes=[pltpu.VMEM((tm, D_MODEL), BF16)],
        compiler_params=_params(2),
        name="ffn",
    )(h, g, w_up, w_down, gf)


def _relative_buckets():
    i = jnp.arange(CHUNK)[:, None]
    j = jnp.arange(2 * CHUNK)[None, :]
    rel = i + CHUNK - j
    relc = jnp.maximum(rel, 0)
    n_exact = N_BUCKETS // 2
    relf = jnp.maximum(relc, n_exact).astype(F32)
    large = n_exact + (jnp.log(relf / n_exact) / math.log(MAX_DISTANCE / n_exact)
                       * (N_BUCKETS - n_exact)).astype(jnp.int32)
    large = jnp.minimum(large, N_BUCKETS - 1)
    bucket = jnp.where(relc < n_exact, relc, large)
    return jnp.where((rel >= 0) & (rel < CHUNK), bucket, -1).astype(jnp.int32)


def kernel(x, rel_bias_table, mix_norm_g, w_in, gate_norm_g, gate_norm_b, w_spatial, b_spatial,
           attn_sinks, out_norm_a_g, out_norm_b_g, w_out, ffn_norm_g, w_up, w_down, final_norm_g):
    bsz, seq, d = x.shape
    depth = w_in.shape[0]
    t = bsz * seq
    bias = _attention_bias(rel_bias_table.astype(F32), _relative_buckets())
    h = x.reshape(t, d)
    for layer in range(depth):
        b_s = jnp.broadcast_to(b_spatial[layer][:, :, None], (A_GROUPS, CHUNK, A_GROUP_DIM))
        h_mid = _mix_block(
            h, mix_norm_g[layer].reshape(1, d), w_in[layer].astype(BF16),
            gate_norm_g[layer].reshape(1, A_WIDTH), gate_norm_b[layer].reshape(1, A_WIDTH),
            w_spatial[layer], b_s, attn_sinks[layer], bias,
            out_norm_a_g[layer].reshape(1, A_WIDTH), out_norm_b_g[layer].reshape(1, B_WIDTH),
            w_out[layer].astype(BF16), tq=256, seq=seq)
        h = _ffn(h_mid, ffn_norm_g[layer].reshape(1, d), w_up, w_down,
                 final_norm_g.reshape(1, d), layer=layer,
                 final_norm=layer == depth - 1, tm=1024, tf=512)
    return h.reshape(bsz, seq, d)
```

```python
import functools
import math

import jax
import jax.numpy as jnp
from jax import lax
from jax.experimental import pallas as pl
from jax.experimental.pallas import tpu as pltpu

D_MODEL = 2048
CHUNK = 128
A_GROUPS = 8
A_GROUP_DIM = 128
A_WIDTH = A_GROUPS * A_GROUP_DIM
HEAD_DIM = 64
B_HEADS = 16
B_KV_HEADS = 2
Q_PER_KV = B_HEADS // B_KV_HEADS
PAIRS_PER_KV = Q_PER_KV // 2
B_WIDTH = B_HEADS * HEAD_DIM
KV_WIDTH = B_KV_HEADS * HEAD_DIM
N_BUCKETS = 32
MAX_DISTANCE = 128
MIX_WIDTH = A_WIDTH + B_WIDTH
D_FF = 4 * D_MODEL
EPS = 1e-5
NEG = -1e30

V7X_VMEM_LIMIT_BYTES = 62 * 1024 * 1024
MXU_COLS = 256
FFN_NORM_FIRST_STEP = 3
MIX_PIECES_PER_CHUNK = A_GROUPS + 1 + B_KV_HEADS * (PAIRS_PER_KV + 2) + 1

BF16 = jnp.bfloat16
F32 = jnp.float32


def _rms_scale(xf):
    return lax.rsqrt(jnp.mean(xf * xf, axis=-1, keepdims=True) + EPS)


def _params(n_axes):
    return pltpu.CompilerParams(
        dimension_semantics=("arbitrary",) * n_axes,
        vmem_limit_bytes=V7X_VMEM_LIMIT_BYTES)


def _const_spec(shape):
    return pl.BlockSpec(shape, lambda *_: (0,) * len(shape), pipeline_mode=pl.Buffered(1))


def _bias_kernel(table_ref, bucket_ref, o_ref):
    h = pl.program_id(0)
    bucket = bucket_ref[...]
    val = jnp.full(bucket.shape, NEG, F32)
    for b in range(N_BUCKETS):
        val = jnp.where(bucket == b, table_ref[b, h], val)
    col = lax.broadcasted_iota(jnp.int32, bucket.shape, 1)
    o_ref[0, 0] = val
    o_ref[1, 0] = jnp.where(col < CHUNK, NEG, val)


def _attention_bias(rel_bias_table, bucket):
    return pl.pallas_call(
        _bias_kernel,
        out_shape=jax.ShapeDtypeStruct(
            (2, B_KV_HEADS, PAIRS_PER_KV * CHUNK, 4 * CHUNK), F32),
        grid=(B_HEADS,),
        in_specs=[
            pl.BlockSpec(memory_space=pltpu.SMEM),
            pl.BlockSpec((CHUNK, 2 * CHUNK), lambda h: (0, 0)),
        ],
        out_specs=pl.BlockSpec(
            (2, 1, CHUNK, 2 * CHUNK),
            lambda h: (0, h // Q_PER_KV, (h % Q_PER_KV) // 2, h % 2)),
        compiler_params=_params(1),
        name="attn_bias",
    )(rel_bias_table, bucket)


def _half_swapped_pair(t):
    lane = lax.broadcasted_iota(jnp.int32, t.shape, 1)
    lo = lane < HEAD_DIM
    tr = pltpu.roll(t, HEAD_DIM, axis=1)
    zero = jnp.zeros_like(t)
    kv0 = jnp.concatenate([jnp.where(lo, t, zero), jnp.where(lo, zero, tr)], axis=0)
    kv1 = jnp.concatenate([jnp.where(lo, tr, zero), jnp.where(lo, zero, t)], axis=0)
    return kv0, kv1


def _mix_block_kernel(n_tiles, chunks_per_seq,
                      x1_ref, x3_ref, g_ref, win_ref, lng_ref, lnb_ref, ws_ref, bs_ref,
                      sinks_ref, bias_ref, ga_ref, gb_ref, wout_ref, wup_ref, wdn_ref,
                      o_ref, wup_bf_ref, wdn_bf_ref,
                      ws_bf, n_buf, u_nx, v_nx, q_nx, kv_nx, u_cu, v_cu, q_cu, kvbuf,
                      mix_nx, mix_cu, a_buf, b_buf):
    tq = x1_ref.shape[0]
    n_chunks = tq // CHUNK
    step = pl.program_id(0)

    def cast_ffn_weights():
        wup_bf_ref[...] = wup_ref[...].astype(BF16)
        wdn_bf_ref[...] = wdn_ref[...].astype(BF16)

    def project_pieces():
        xf = x1_ref[...]
        n_buf[...] = (xf * _rms_scale(xf) * g_ref[...]).astype(BF16)
        col = 0
        for dst in (u_nx, v_nx, q_nx, kv_nx):
            for c0 in range(0, dst.shape[1], MXU_COLS):
                def piece(dst=dst, c0=c0, col=col):
                    dst[:, c0:c0 + MXU_COLS] = jnp.dot(
                        n_buf[...], win_ref[:, col + c0:col + c0 + MXU_COLS],
                        preferred_element_type=F32).astype(BF16)
                yield piece
            col += dst.shape[1]

    def take_projection():
        kvbuf[0:CHUNK, :] = kvbuf[tq:tq + CHUNK, :]
        kvbuf[CHUNK:CHUNK + tq, :] = kv_nx[...]
        u_cu[...] = u_nx[...]
        v_cu[...] = v_nx[...]
        q_cu[...] = q_nx[...]

    def mix_a(c):
        rows = slice(c * CHUNK, (c + 1) * CHUNK)
        ss_a = jnp.zeros((CHUNK, A_GROUP_DIM), F32)
        for g in range(A_GROUPS):
            cols = slice(g * A_GROUP_DIM, (g + 1) * A_GROUP_DIM)
            ug = jax.nn.gelu(u_cu[rows, cols].astype(F32))
            vg = jax.nn.gelu(v_cu[rows, cols].astype(F32))
            mu = jnp.mean(vg, axis=-1, keepdims=True)
            vc = vg - mu
            var = jnp.mean(vc * vc, axis=-1, keepdims=True)
            vn = vc * lax.rsqrt(var + EPS) * lng_ref[:, cols] + lnb_ref[:, cols]
            mixed = jnp.dot(ws_bf[g], vn.astype(BF16), preferred_element_type=F32) + bs_ref[g]
            ag = ug * mixed
            a_buf[c, :, cols] = ag
            ss_a = ss_a + ag * ag
            yield
        scale_a = lax.rsqrt(jnp.sum(ss_a, axis=-1, keepdims=True) * (1.0 / A_WIDTH) + EPS)
        mix_nx[rows, 0:A_WIDTH] = (a_buf[c] * scale_a * ga_ref[...]).astype(BF16)
        yield

    def mix_b(c):
        rows = slice(c * CHUNK, (c + 1) * CHUNK)
        band = kvbuf[c * CHUNK:(c + 2) * CHUNK, :]
        k_stacks = _half_swapped_pair(band[:, :KV_WIDTH].astype(F32) * (HEAD_DIM ** -0.5))
        v_stacks = _half_swapped_pair(band[:, KV_WIDTH:].astype(F32))
        chunk_id = (step - 1) * n_chunks + c
        first = (chunk_id % chunks_per_seq == 0).astype(jnp.int32)
        lane = lax.broadcasted_iota(jnp.int32, (CHUNK, 2 * HEAD_DIM), 1)
        lo = lane < HEAD_DIM
        ss_b = jnp.zeros((CHUNK, 2 * HEAD_DIM), F32)
        for g in range(B_KV_HEADS):
            q_stack = jnp.concatenate(
                [q_cu[rows, (PAIRS_PER_KV * g + p) * 128:(PAIRS_PER_KV * g + p + 1) * 128]
                 for p in range(PAIRS_PER_KV)], axis=0)
            s_all = lax.dot_general(q_stack, k_stacks[g].astype(BF16),
                                    (((1,), (1,)), ((), ())),
                                    preferred_element_type=F32)
            yield
            p_rows = []
            rdens = []
            for p in range(PAIRS_PER_KV):
                p_cols = []
                rden_pair = []
                for par in range(2):
                    head = Q_PER_KV * g + 2 * p + par
                    rs = slice(p * CHUNK, (p + 1) * CHUNK)
                    cs = slice(par * 2 * CHUNK, (par + 1) * 2 * CHUNK)
                    s = s_all[rs, cs] + bias_ref[first, g, rs, cs]
                    sink = sinks_ref[head]
                    m = jnp.maximum(jnp.max(s, axis=-1, keepdims=True), sink)
                    e = jnp.exp(s - m)
                    den = jnp.sum(e, axis=-1, keepdims=True) + jnp.exp(sink - m)
                    p_cols.append(e.astype(BF16))
                    rden_pair.append(1.0 / den)
                p_rows.append(jnp.concatenate(p_cols, axis=1))
                rdens.append(rden_pair)
                yield
            probs = jnp.concatenate(p_rows, axis=0)
            o_all = jnp.dot(probs, v_stacks[g].astype(BF16), preferred_element_type=F32)
            for p in range(PAIRS_PER_KV):
                blk = PAIRS_PER_KV * g + p
                o = o_all[p * CHUNK:(p + 1) * CHUNK] * jnp.where(lo, rdens[p][0], rdens[p][1])
                b_buf[c, :, blk * 128:(blk + 1) * 128] = o
                ss_b = ss_b + o * o
            yield
        scale_b = lax.rsqrt(jnp.sum(ss_b, axis=-1, keepdims=True) * (1.0 / B_WIDTH) + EPS)
        mix_nx[rows, A_WIDTH:MIX_WIDTH] = (b_buf[c] * scale_b * gb_ref[...]).astype(BF16)
        yield

    def mix_pieces():
        for c in range(n_chunks):
            gens = [mix_b(c), mix_a(c)]
            while gens:
                for gen in list(gens):
                    try:
                        next(gen)
                        yield
                    except StopIteration:
                        gens.remove(gen)

    def output_pieces():
        mix_cu[...] = mix_nx[...]
        for c0 in range(0, D_MODEL, MXU_COLS):
            def piece(c0=c0):
                o_ref[:, c0:c0 + MXU_COLS] = x3_ref[:, c0:c0 + MXU_COLS] + jnp.dot(
                    mix_cu[...], wout_ref[:, c0:c0 + MXU_COLS], preferred_element_type=F32)
            yield piece

    def run(stage1, stage2, stage3):
        matmuls = []
        if stage3:
            matmuls += list(output_pieces())
        if stage2:
            take_projection()
        if stage1:
            matmuls += list(project_pieces())
        matmuls[0]()
        cast_ffn_weights()
        if not stage2:
            for piece in matmuls[1:]:
                piece()
            return
        n_yields = n_chunks * MIX_PIECES_PER_CHUNK
        done = 1
        for k, _ in enumerate(mix_pieces(), start=1):
            want = 1 + (k * (len(matmuls) - 1)) // n_yields
            while done < want:
                matmuls[done]()
                done += 1
        assert done == len(matmuls), (done, len(matmuls))

    @pl.when(step == 0)
    def _():
        row = lax.broadcasted_iota(jnp.int32, (CHUNK, CHUNK), 0)
        col = lax.broadcasted_iota(jnp.int32, (CHUNK, CHUNK), 1)
        for g in range(A_GROUPS):
            ws_bf[g] = jnp.where(row >= col, ws_ref[g], 0.0).astype(BF16)
        kvbuf[...] = jnp.zeros_like(kvbuf)
        run(True, False, False)

    @pl.when(step == 1)
    def _():
        run(True, True, False)

    @pl.when((step >= 2) & (step < n_tiles))
    def _():
        run(True, True, True)

    @pl.when(step == n_tiles)
    def _():
        run(False, True, True)

    @pl.when(step == n_tiles + 1)
    def _():
        run(False, False, True)


def _mix_block(x2, g, w_in, lng, lnb, w_s, b_s, sinks, bias, ga, gb, w_out, w_up, w_down,
               layer, tq, seq):
    t = x2.shape[0]
    n_tiles = t // tq
    n_chunks = tq // CHUNK
    proj = lambda w: pltpu.VMEM((tq, w), BF16)
    out_tile = lambda s: (jnp.clip(s - 2, 0, n_tiles - 1), 0)
    up_rows, dn_rows = D_MODEL // n_tiles, D_FF // n_tiles
    slab = lambda s: (jnp.minimum(s, n_tiles - 1), 0)
    return pl.pallas_call(
        functools.partial(_mix_block_kernel, n_tiles, seq // CHUNK),
        out_shape=[jax.ShapeDtypeStruct((t, D_MODEL), F32),
                   jax.ShapeDtypeStruct((D_MODEL, D_FF), BF16),
                   jax.ShapeDtypeStruct((D_FF, D_MODEL), BF16)],
        grid=(n_tiles + 2,),
        in_specs=[
            pl.BlockSpec((tq, D_MODEL), lambda s: (jnp.minimum(s, n_tiles - 1), 0)),
            pl.BlockSpec((tq, D_MODEL), out_tile),
            _const_spec((1, D_MODEL)),
            _const_spec(w_in.shape),
            _const_spec((1, A_WIDTH)),
            _const_spec((1, A_WIDTH)),
            _const_spec(w_s.shape),
            _const_spec(b_s.shape),
            pl.BlockSpec(memory_space=pltpu.SMEM),
            _const_spec(bias.shape),
            _const_spec((1, A_WIDTH)),
            _const_spec((1, B_WIDTH)),
            _const_spec(w_out.shape),
            pl.BlockSpec((None, up_rows, D_FF), lambda s: (layer,) + slab(s)),
            pl.BlockSpec((None, dn_rows, D_MODEL), lambda s: (layer,) + slab(s)),
        ],
        out_specs=[pl.BlockSpec((tq, D_MODEL), out_tile),
                   pl.BlockSpec((up_rows, D_FF), slab),
                   pl.BlockSpec((dn_rows, D_MODEL), slab)],
        scratch_shapes=[
            pltpu.VMEM((A_GROUPS, CHUNK, CHUNK), BF16),
            proj(D_MODEL),
            proj(A_WIDTH), proj(A_WIDTH), proj(B_WIDTH), proj(2 * KV_WIDTH),
            proj(A_WIDTH), proj(A_WIDTH), proj(B_WIDTH),
            pltpu.VMEM((tq + CHUNK, 2 * KV_WIDTH), BF16),
            proj(MIX_WIDTH), proj(MIX_WIDTH),
            pltpu.VMEM((n_chunks, CHUNK, A_WIDTH), F32),
            pltpu.VMEM((n_chunks, CHUNK, B_WIDTH), F32),
        ],
        compiler_params=_params(1),
        name="mix_block",
    )(x2, x2, g, w_in, lng, lnb, w_s, b_s, sinks, bias, ga, gb, w_out, w_up, w_down)


def _ffn_kernel(final_norm, norm_chunks, h_hbm, g_ref, wu_ref, wd_ref, gf_ref, o_hbm,
                acc, nbuf, sem_in, sem_out):
    i = pl.program_id(0)
    f = pl.program_id(1)
    n_tiles = pl.num_programs(0)
    last_f = pl.num_programs(1) - 1
    tm = acc.shape[1]
    chunk_rows = tm // norm_chunks
    slot = i % 2
    other = 1 - slot

    def h_copy(tile, s):
        return pltpu.make_async_copy(h_hbm.at[pl.ds(tile * tm, tm)], acc.at[s], sem_in)

    def o_copy(tile, s):
        return pltpu.make_async_copy(acc.at[s], o_hbm.at[pl.ds(tile * tm, tm)], sem_out.at[s])

    def normalize(src_slot, dst_slot, r0, rows):
        hf = acc[src_slot, pl.ds(r0, rows), :]
        nbuf[dst_slot, pl.ds(r0, rows), :] = (hf * _rms_scale(hf) * g_ref[...]).astype(BF16)

    @pl.when((i == 0) & (f == 0))
    def _():
        h_copy(0, 0).start()
        h_copy(0, 0).wait()
        for k in range(norm_chunks):
            normalize(0, 0, k * chunk_rows, chunk_rows)

    @pl.when((f == 1) & (i + 1 < n_tiles))
    def _():
        @pl.when(i >= 1)
        def _():
            o_copy(i - 1, other).wait()
        h_copy(i + 1, other).start()

    @pl.when((f == FFN_NORM_FIRST_STEP - 1) & (i + 1 < n_tiles))
    def _():
        h_copy(i + 1, other).wait()

    def step(cur, norm_next, last):
        nxt = 1 - cur
        z = jnp.maximum(jnp.dot(nbuf[cur], wu_ref[...], preferred_element_type=F32), 0.0)
        if norm_next:
            k = jnp.clip(f - FFN_NORM_FIRST_STEP, 0, norm_chunks - 1)
            normalize(nxt, nxt, pl.multiple_of(k * chunk_rows, chunk_rows), chunk_rows)
        update = jnp.dot((z * z).astype(BF16), wd_ref[...], preferred_element_type=F32)
        y = acc[cur] + update
        if last and final_norm:
            y = y * _rms_scale(y) * gf_ref[...]
        acc[cur] = y

    for cur in (0, 1):
        mine = slot == cur

        @pl.when(mine & (f < FFN_NORM_FIRST_STEP))
        def _():
            step(cur, False, False)

        @pl.when(mine & (f >= FFN_NORM_FIRST_STEP) & (f < last_f))
        def _():
            step(cur, True, False)

        @pl.when(mine & (f == last_f))
        def _():
            step(cur, True, True)

    @pl.when(f == last_f)
    def _():
        o_copy(i, slot).start()

        @pl.when(i == n_tiles - 1)
        def _():
            @pl.when(i >= 1)
            def _():
                o_copy(i - 1, other).wait()
            o_copy(i, slot).wait()


def _ffn(h, g, w_up, w_down, gf, final_norm, tm, tf):
    t = h.shape[0]
    n_f = D_FF // tf
    norm_chunks = 1 << ((n_f - FFN_NORM_FIRST_STEP).bit_length() - 1)
    assert norm_chunks >= 1 and tm % norm_chunks == 0
    return pl.pallas_call(
        functools.partial(_ffn_kernel, final_norm, norm_chunks),
        out_shape=jax.ShapeDtypeStruct((t, D_MODEL), F32),
        grid=(t // tm, n_f),
        in_specs=[
            pl.BlockSpec(memory_space=pl.ANY),
            _const_spec((1, D_MODEL)),
            pl.BlockSpec((D_MODEL, tf), lambda i, f: (0, f)),
            pl.BlockSpec((tf, D_MODEL), lambda i, f: (f, 0)),
            _const_spec((1, D_MODEL)),
        ],
        out_specs=pl.BlockSpec(memory_space=pl.ANY),
        scratch_shapes=[
            pltpu.VMEM((2, tm, D_MODEL), F32),
            pltpu.VMEM((2, tm, D_MODEL), BF16),
            pltpu.SemaphoreType.DMA(()),
            pltpu.SemaphoreType.DMA((2,)),
        ],
        compiler_params=_params(2),
        name="ffn",
    )(h, g, w_up, w_down, gf)


def _relative_buckets():
    i = jnp.arange(CHUNK)[:, None]
    j = jnp.arange(2 * CHUNK)[None, :]
    rel = i + CHUNK - j
    relc = jnp.maximum(rel, 0)
    n_exact = N_BUCKETS // 2
    relf = jnp.maximum(relc, n_exact).astype(F32)
    large = n_exact + (jnp.log(relf / n_exact) / math.log(MAX_DISTANCE / n_exact)
                       * (N_BUCKETS - n_exact)).astype(jnp.int32)
    large = jnp.minimum(large, N_BUCKETS - 1)
    bucket = jnp.where(relc < n_exact, relc, large)
    return jnp.where((rel >= 0) & (rel < CHUNK), bucket, -1).astype(jnp.int32)


def kernel(x, rel_bias_table, mix_norm_g, w_in, gate_norm_g, gate_norm_b, w_spatial, b_spatial,
           attn_sinks, out_norm_a_g, out_norm_b_g, w_out, ffn_norm_g, w_up, w_down, final_norm_g):
    bsz, seq, d = x.shape
    depth = w_in.shape[0]
    t = bsz * seq
    bias = _attention_bias(rel_bias_table.astype(F32), _relative_buckets())
    h = x.reshape(t, d)
    for layer in range(depth):
        b_s = jnp.broadcast_to(b_spatial[layer][:, :, None], (A_GROUPS, CHUNK, A_GROUP_DIM))
        h_mid, w_up_bf, w_down_bf = _mix_block(
            h, mix_norm_g[layer].reshape(1, d), w_in[layer].astype(BF16),
            gate_norm_g[layer].reshape(1, A_WIDTH), gate_norm_b[layer].reshape(1, A_WIDTH),
            w_spatial[layer], b_s, attn_sinks[layer], bias,
            out_norm_a_g[layer].reshape(1, A_WIDTH), out_norm_b_g[layer].reshape(1, B_WIDTH),
            w_out[layer].astype(BF16), w_up, w_down, layer=layer, tq=256, seq=seq)
        h = _ffn(h_mid, ffn_norm_g[layer].reshape(1, d), w_up_bf, w_down_bf,
                 final_norm_g.reshape(1, d), final_norm=layer == depth - 1, tm=1024, tf=1024)
    return h.reshape(bsz, seq, d)
```

```python
import functools
import math

import jax
import jax.numpy as jnp
from jax import lax
from jax.experimental import pallas as pl
from jax.experimental.pallas import tpu as pltpu

D_MODEL = 2048
CHUNK = 128
A_GROUPS = 8
A_GROUP_DIM = 128
A_WIDTH = A_GROUPS * A_GROUP_DIM
HEAD_DIM = 64
B_HEADS = 16
B_KV_HEADS = 2
Q_PER_KV = B_HEADS // B_KV_HEADS
PAIRS_PER_KV = Q_PER_KV // 2
B_WIDTH = B_HEADS * HEAD_DIM
KV_WIDTH = B_KV_HEADS * HEAD_DIM
N_BUCKETS = 32
MAX_DISTANCE = 128
MIX_WIDTH = A_WIDTH + B_WIDTH
D_FF = 4 * D_MODEL
EPS = 1e-5
NEG = -1e30

V7X_VMEM_LIMIT_BYTES = 62 * 1024 * 1024
MXU_COLS = 256
FFN_NORM_FIRST_STEP = 3
MIX_PIECES_PER_CHUNK = A_GROUPS + 1 + B_KV_HEADS * (PAIRS_PER_KV + 2) + 1

BF16 = jnp.bfloat16
F32 = jnp.float32


def _rms_scale(xf):
    return lax.rsqrt(jnp.mean(xf * xf, axis=-1, keepdims=True) + EPS)


def _params(n_axes):
    return pltpu.CompilerParams(
        dimension_semantics=("arbitrary",) * n_axes,
        vmem_limit_bytes=V7X_VMEM_LIMIT_BYTES)


def _const_spec(shape):
    return pl.BlockSpec(shape, lambda *_: (0,) * len(shape), pipeline_mode=pl.Buffered(1))


def _bias_kernel(table_ref, bucket_ref, o_ref):
    h = pl.program_id(0)
    bucket = bucket_ref[...]
    val = jnp.full(bucket.shape, NEG, F32)
    for b in range(N_BUCKETS):
        val = jnp.where(bucket == b, table_ref[b, h], val)
    col = lax.broadcasted_iota(jnp.int32, bucket.shape, 1)
    o_ref[0, 0] = val
    o_ref[1, 0] = jnp.where(col < CHUNK, NEG, val)


def _attention_bias(rel_bias_table, bucket):
    return pl.pallas_call(
        _bias_kernel,
        out_shape=jax.ShapeDtypeStruct(
            (2, B_KV_HEADS, PAIRS_PER_KV * CHUNK, 4 * CHUNK), F32),
        grid=(B_HEADS,),
        in_specs=[
            pl.BlockSpec(memory_space=pltpu.SMEM),
            pl.BlockSpec((CHUNK, 2 * CHUNK), lambda h: (0, 0)),
        ],
        out_specs=pl.BlockSpec(
            (2, 1, CHUNK, 2 * CHUNK),
            lambda h: (0, h // Q_PER_KV, (h % Q_PER_KV) // 2, h % 2)),
        compiler_params=_params(1),
        name="attn_bias",
    )(rel_bias_table, bucket)


def _half_swapped_pair(t):
    lane = lax.broadcasted_iota(jnp.int32, t.shape, 1)
    lo = lane < HEAD_DIM
    tr = pltpu.roll(t, HEAD_DIM, axis=1)
    zero = jnp.zeros_like(t)
    kv0 = jnp.concatenate([jnp.where(lo, t, zero), jnp.where(lo, zero, tr)], axis=0)
    kv1 = jnp.concatenate([jnp.where(lo, tr, zero), jnp.where(lo, zero, t)], axis=0)
    return kv0, kv1


def _mix_block_kernel(n_tiles, chunks_per_seq,
                      x1_ref, x3_ref, g_ref, win_ref, lng_ref, lnb_ref, ws_ref, bs_ref,
                      sinks_ref, bias_ref, ga_ref, gb_ref, wout_ref, wup_ref, wdn_ref,
                      o_ref, wup_bf_ref, wdn_bf_ref,
                      ws_bf, n_buf, u_nx, v_nx, q_nx, kv_nx, u_cu, v_cu, q_cu, kvbuf,
                      mix_nx, mix_cu, a_buf, b_buf):
    tq = x1_ref.shape[0]
    n_chunks = tq // CHUNK
    step = pl.program_id(0)

    def cast_ffn_weights():
        wup_bf_ref[...] = wup_ref[...].astype(BF16)
        wdn_bf_ref[...] = wdn_ref[...].astype(BF16)

    def project_pieces():
        xf = x1_ref[...]
        n_buf[...] = (xf * _rms_scale(xf) * g_ref[...]).astype(BF16)
        col = 0
        for dst in (u_nx, v_nx, q_nx, kv_nx):
            for c0 in range(0, dst.shape[1], MXU_COLS):
                def piece(dst=dst, c0=c0, col=col):
                    dst[:, c0:c0 + MXU_COLS] = jnp.dot(
                        n_buf[...], win_ref[:, col + c0:col + c0 + MXU_COLS],
                        preferred_element_type=F32).astype(BF16)
                yield piece
            col += dst.shape[1]

    def take_projection():
        kvbuf[0:CHUNK, :] = kvbuf[tq:tq + CHUNK, :]
        kvbuf[CHUNK:CHUNK + tq, :] = kv_nx[...]
        u_cu[...] = u_nx[...]
        v_cu[...] = v_nx[...]
        q_cu[...] = q_nx[...]

    def mix_a(c):
        rows = slice(c * CHUNK, (c + 1) * CHUNK)
        ss_a = jnp.zeros((CHUNK, A_GROUP_DIM), F32)
        for g in range(A_GROUPS):
            cols = slice(g * A_GROUP_DIM, (g + 1) * A_GROUP_DIM)
            ug = jax.nn.gelu(u_cu[rows, cols].astype(F32))
            vg = jax.nn.gelu(v_cu[rows, cols].astype(F32))
            mu = jnp.mean(vg, axis=-1, keepdims=True)
            vc = vg - mu
            var = jnp.mean(vc * vc, axis=-1, keepdims=True)
            vn = vc * lax.rsqrt(var + EPS) * lng_ref[:, cols] + lnb_ref[:, cols]
            mixed = jnp.dot(ws_bf[g], vn.astype(BF16), preferred_element_type=F32) + bs_ref[g]
            ag = ug * mixed
            a_buf[c, :, cols] = ag
            ss_a = ss_a + ag * ag
            yield
        scale_a = lax.rsqrt(jnp.sum(ss_a, axis=-1, keepdims=True) * (1.0 / A_WIDTH) + EPS)
        mix_nx[rows, 0:A_WIDTH] = (a_buf[c] * scale_a * ga_ref[...]).astype(BF16)
        yield

    def mix_b(c):
        rows = slice(c * CHUNK, (c + 1) * CHUNK)
        band = kvbuf[c * CHUNK:(c + 2) * CHUNK, :]
        k_stacks = _half_swapped_pair(band[:, :KV_WIDTH].astype(F32) * (HEAD_DIM ** -0.5))
        v_stacks = _half_swapped_pair(band[:, KV_WIDTH:].astype(F32))
        chunk_id = (step - 1) * n_chunks + c
        first = (chunk_id % chunks_per_seq == 0).astype(jnp.int32)
        lane = lax.broadcasted_iota(jnp.int32, (CHUNK, 2 * HEAD_DIM), 1)
        lo = lane < HEAD_DIM
        ss_b = jnp.zeros((CHUNK, 2 * HEAD_DIM), F32)
        for g in range(B_KV_HEADS):
            q_stack = jnp.concatenate(
                [q_cu[rows, (PAIRS_PER_KV * g + p) * 128:(PAIRS_PER_KV * g + p + 1) * 128]
                 for p in range(PAIRS_PER_KV)], axis=0)
            s_all = lax.dot_general(q_stack, k_stacks[g].astype(BF16),
                                    (((1,), (1,)), ((), ())),
                                    preferred_element_type=F32)
            yield
            p_rows = []
            rdens = []
            for p in range(PAIRS_PER_KV):
                p_cols = []
                rden_pair = []
                for par in range(2):
                    head = Q_PER_KV * g + 2 * p + par
                    rs = slice(p * CHUNK, (p + 1) * CHUNK)
                    cs = slice(par * 2 * CHUNK, (par + 1) * 2 * CHUNK)
                    s = s_all[rs, cs] + bias_ref[first, g, rs, cs]
                    sink = sinks_ref[head]
                    m = jnp.maximum(jnp.max(s, axis=-1, keepdims=True), sink)
                    e = jnp.exp(s - m)
                    den = jnp.sum(e, axis=-1, keepdims=True) + jnp.exp(sink - m)
                    p_cols.append(e.astype(BF16))
                    rden_pair.append(1.0 / den)
                p_rows.append(jnp.concatenate(p_cols, axis=1))
                rdens.append(rden_pair)
                yield
            probs = jnp.concatenate(p_rows, axis=0)
            o_all = jnp.dot(probs, v_stacks[g].astype(BF16), preferred_element_type=F32)
            for p in range(PAIRS_PER_KV):
                blk = PAIRS_PER_KV * g + p
                o = o_all[p * CHUNK:(p + 1) * CHUNK] * jnp.where(lo, rdens[p][0], rdens[p][1])
                b_buf[c, :, blk * 128:(blk + 1) * 128] = o
                ss_b = ss_b + o * o
            yield
        scale_b = lax.rsqrt(jnp.sum(ss_b, axis=-1, keepdims=True) * (1.0 / B_WIDTH) + EPS)
        mix_nx[rows, A_WIDTH:MIX_WIDTH] = (b_buf[c] * scale_b * gb_ref[...]).astype(BF16)
        yield

    def mix_pieces():
        for c in range(n_chunks):
            gens = [mix_b(c), mix_a(c)]
            while gens:
                for gen in list(gens):
                    try:
                        next(gen)
                        yield
                    except StopIteration:
                        gens.remove(gen)

    def output_pieces():
        mix_cu[...] = mix_nx[...]
        for c0 in range(0, D_MODEL, MXU_COLS):
            def piece(c0=c0):
                o_ref[:, c0:c0 + MXU_COLS] = x3_ref[:, c0:c0 + MXU_COLS] + jnp.dot(
                    mix_cu[...], wout_ref[:, c0:c0 + MXU_COLS], preferred_element_type=F32)
            yield piece

    def run(stage1, stage2, stage3):
        matmuls = []
        if stage3:
            matmuls += list(output_pieces())
        if stage2:
            take_projection()
        if stage1:
            matmuls += list(project_pieces())
        matmuls[0]()
        cast_ffn_weights()
        if not stage2:
            for piece in matmuls[1:]:
                piece()
            return
        n_yields = n_chunks * MIX_PIECES_PER_CHUNK
        done = 1
        for k, _ in enumerate(mix_pieces(), start=1):
            want = 1 + (k * (len(matmuls) - 1)) // n_yields
            while done < want:
                matmuls[done]()
                done += 1
        assert done == len(matmuls), (done, len(matmuls))

    @pl.when(step == 0)
    def _():
        row = lax.broadcasted_iota(jnp.int32, (CHUNK, CHUNK), 0)
        col = lax.broadcasted_iota(jnp.int32, (CHUNK, CHUNK), 1)
        for g in range(A_GROUPS):
            ws_bf[g] = jnp.where(row >= col, ws_ref[g], 0.0).astype(BF16)
        kvbuf[...] = jnp.zeros_like(kvbuf)
        run(True, False, False)

    @pl.when(step == 1)
    def _():
        run(True, True, False)

    @pl.when((step >= 2) & (step < n_tiles))
    def _():
        run(True, True, True)

    @pl.when(step == n_tiles)
    def _():
        run(False, True, True)

    @pl.when(step == n_tiles + 1)
    def _():
        run(False, False, True)


def _mix_block(x2, g, w_in, lng, lnb, w_s, b_s, sinks, bias, ga, gb, w_out, w_up, w_down,
               layer, tq, seq):
    t = x2.shape[0]
    n_tiles = t // tq
    n_chunks = tq // CHUNK
    proj = lambda w: pltpu.VMEM((tq, w), BF16)
    out_tile = lambda s: (jnp.clip(s - 2, 0, n_tiles - 1), 0)
    up_rows, dn_rows = D_MODEL // n_tiles, D_FF // n_tiles
    slab = lambda s: (jnp.minimum(s, n_tiles - 1), 0)
    return pl.pallas_call(
        functools.partial(_mix_block_kernel, n_tiles, seq // CHUNK),
        out_shape=[jax.ShapeDtypeStruct((t, D_MODEL), F32),
                   jax.ShapeDtypeStruct((D_MODEL, D_FF), BF16),
                   jax.ShapeDtypeStruct((D_FF, D_MODEL), BF16)],
        grid=(n_tiles + 2,),
        in_specs=[
            pl.BlockSpec((tq, D_MODEL), lambda s: (jnp.minimum(s, n_tiles - 1), 0)),
            pl.BlockSpec((tq, D_MODEL), out_tile),
            _const_spec((1, D_MODEL)),
            _const_spec(w_in.shape),
            _const_spec((1, A_WIDTH)),
            _const_spec((1, A_WIDTH)),
            _const_spec(w_s.shape),
            _const_spec(b_s.shape),
            pl.BlockSpec(memory_space=pltpu.SMEM),
            _const_spec(bias.shape),
            _const_spec((1, A_WIDTH)),
            _const_spec((1, B_WIDTH)),
            _const_spec(w_out.shape),
            pl.BlockSpec((None, up_rows, D_FF), lambda s: (layer,) + slab(s)),
            pl.BlockSpec((None, dn_rows, D_MODEL), lambda s: (layer,) + slab(s)),
        ],
        out_specs=[pl.BlockSpec((tq, D_MODEL), out_tile),
                   pl.BlockSpec((up_rows, D_FF), slab),
                   pl.BlockSpec((dn_rows, D_MODEL), slab)],
        scratch_shapes=[
            pltpu.VMEM((A_GROUPS, CHUNK, CHUNK), BF16),
            proj(D_MODEL),
            proj(A_WIDTH), proj(A_WIDTH), proj(B_WIDTH), proj(2 * KV_WIDTH),
            proj(A_WIDTH), proj(A_WIDTH), proj(B_WIDTH),
            pltpu.VMEM((tq + CHUNK, 2 * KV_WIDTH), BF16),
            proj(MIX_WIDTH), proj(MIX_WIDTH),
            pltpu.VMEM((n_chunks, CHUNK, A_WIDTH), F32),
            pltpu.VMEM((n_chunks, CHUNK, B_WIDTH), F32),
        ],
        compiler_params=_params(1),
        name="mix_block",
    )(x2, x2, g, w_in, lng, lnb, w_s, b_s, sinks, bias, ga, gb, w_out, w_up, w_down)


def _ffn_kernel(final_norm, norm_chunks, h_hbm, g_ref, wu_ref, wd_ref, gf_ref, o_hbm,
                acc, nbuf, nstage, sem_in, sem_out):
    i = pl.program_id(0)
    f = pl.program_id(1)
    n_tiles = pl.num_programs(0)
    last_f = pl.num_programs(1) - 1
    tm = acc.shape[1]
    chunk_rows = tm // norm_chunks
    slot = i % 2
    other = 1 - slot

    def h_copy(tile, s):
        return pltpu.make_async_copy(h_hbm.at[pl.ds(tile * tm, tm)], acc.at[s], sem_in)

    def o_copy(tile, s):
        return pltpu.make_async_copy(acc.at[s], o_hbm.at[pl.ds(tile * tm, tm)], sem_out.at[s])

    def normalize(src_slot, r0, dst):
        rows = pl.ds(r0, chunk_rows)
        hf = acc[src_slot, rows, :]
        dst[rows, :] = (hf * _rms_scale(hf) * g_ref[...]).astype(BF16)

    @pl.when((i == 0) & (f == 0))
    def _():
        h_copy(0, 0).start()
        h_copy(0, 0).wait()
        for k in range(norm_chunks):
            normalize(0, k * chunk_rows, nbuf)

    @pl.when((i > 0) & (f == 0))
    def _():
        nbuf[...] = nstage[...]

    @pl.when((f == 1) & (i + 1 < n_tiles))
    def _():
        @pl.when(i >= 1)
        def _():
            o_copy(i - 1, other).wait()
        h_copy(i + 1, other).start()

    @pl.when((f == FFN_NORM_FIRST_STEP - 1) & (i + 1 < n_tiles))
    def _():
        h_copy(i + 1, other).wait()

    z = jnp.maximum(jnp.dot(nbuf[...], wu_ref[...], preferred_element_type=F32), 0.0)
    ready = f >= FFN_NORM_FIRST_STEP
    k = jnp.clip(f - FFN_NORM_FIRST_STEP, 0, norm_chunks - 1)
    normalize(jnp.where(ready, other, slot), pl.multiple_of(k * chunk_rows, chunk_rows), nstage)
    update = jnp.dot((z * z).astype(BF16), wd_ref[...], preferred_element_type=F32)
    acc[slot] = acc[slot] + update

    @pl.when(f == last_f)
    def _():
        if final_norm:
            def norm_rows(k, carry):
                rows = pl.ds(pl.multiple_of(k * chunk_rows, chunk_rows), chunk_rows)
                y = acc[slot, rows, :]
                acc[slot, rows, :] = y * _rms_scale(y) * gf_ref[...]
                return carry
            lax.fori_loop(0, norm_chunks, norm_rows, 0)
        o_copy(i, slot).start()

        @pl.when(i == n_tiles - 1)
        def _():
            @pl.when(i >= 1)
            def _():
                o_copy(i - 1, other).wait()
            o_copy(i, slot).wait()


def _ffn(h, g, w_up, w_down, gf, final_norm, tm, tf):
    t = h.shape[0]
    n_f = D_FF // tf
    norm_chunks = 1 << ((n_f - FFN_NORM_FIRST_STEP).bit_length() - 1)
    assert norm_chunks >= 1 and tm % norm_chunks == 0
    return pl.pallas_call(
        functools.partial(_ffn_kernel, final_norm, norm_chunks),
        out_shape=jax.ShapeDtypeStruct((t, D_MODEL), F32),
        grid=(t // tm, n_f),
        in_specs=[
            pl.BlockSpec(memory_space=pl.ANY),
            _const_spec((1, D_MODEL)),
            pl.BlockSpec((D_MODEL, tf), lambda i, f: (0, f)),
            pl.BlockSpec((tf, D_MODEL), lambda i, f: (f, 0)),
            _const_spec((1, D_MODEL)),
        ],
        out_specs=pl.BlockSpec(memory_space=pl.ANY),
        scratch_shapes=[
            pltpu.VMEM((2, tm, D_MODEL), F32),
            pltpu.VMEM((tm, D_MODEL), BF16),
            pltpu.VMEM((tm, D_MODEL), BF16),
            pltpu.SemaphoreType.DMA(()),
            pltpu.SemaphoreType.DMA((2,)),
        ],
        compiler_params=_params(2),
        name="ffn",
    )(h, g, w_up, w_down, gf)


def _relative_buckets():
    i = jnp.arange(CHUNK)[:, None]
    j = jnp.arange(2 * CHUNK)[None, :]
    rel = i + CHUNK - j
    relc = jnp.maximum(rel, 0)
    n_exact = N_BUCKETS // 2
    relf = jnp.maximum(relc, n_exact).astype(F32)
    large = n_exact + (jnp.log(relf / n_exact) / math.log(MAX_DISTANCE / n_exact)
                       * (N_BUCKETS - n_exact)).astype(jnp.int32)
    large = jnp.minimum(large, N_BUCKETS - 1)
    bucket = jnp.where(relc < n_exact, relc, large)
    return jnp.where((rel >= 0) & (rel < CHUNK), bucket, -1).astype(jnp.int32)


def kernel(x, rel_bias_table, mix_norm_g, w_in, gate_norm_g, gate_norm_b, w_spatial, b_spatial,
           attn_sinks, out_norm_a_g, out_norm_b_g, w_out, ffn_norm_g, w_up, w_down, final_norm_g):
    bsz, seq, d = x.shape
    depth = w_in.shape[0]
    t = bsz * seq
    bias = _attention_bias(rel_bias_table.astype(F32), _relative_buckets())
    h = x.reshape(t, d)
    for layer in range(depth):
        b_s = jnp.broadcast_to(b_spatial[layer][:, :, None], (A_GROUPS, CHUNK, A_GROUP_DIM))
        h_mid, w_up_bf, w_down_bf = _mix_block(
            h, mix_norm_g[layer].reshape(1, d), w_in[layer].astype(BF16),
            gate_norm_g[layer].reshape(1, A_WIDTH), gate_norm_b[layer].reshape(1, A_WIDTH),
            w_spatial[layer], b_s, attn_sinks[layer], bias,
            out_norm_a_g[layer].reshape(1, A_WIDTH), out_norm_b_g[layer].reshape(1, B_WIDTH),
            w_out[layer].astype(BF16), w_up, w_down, layer=layer, tq=256, seq=seq)
        h = _ffn(h_mid, ffn_norm_g[layer].reshape(1, d), w_up_bf, w_down_bf,
                 final_norm_g.reshape(1, d), final_norm=layer == depth - 1, tm=1024, tf=1024)
    return h.reshape(bsz, seq, d)
```

```python
import functools
import math

import jax
import jax.numpy as jnp
from jax import lax
from jax.experimental import pallas as pl
from jax.experimental.pallas import tpu as pltpu

D_MODEL = 2048
CHUNK = 128
A_GROUPS = 8
A_GROUP_DIM = 128
A_WIDTH = A_GROUPS * A_GROUP_DIM
HEAD_DIM = 64
B_HEADS = 16
B_KV_HEADS = 2
Q_PER_KV = B_HEADS // B_KV_HEADS
PAIRS_PER_KV = Q_PER_KV // 2
B_WIDTH = B_HEADS * HEAD_DIM
KV_WIDTH = B_KV_HEADS * HEAD_DIM
N_BUCKETS = 32
MAX_DISTANCE = 128
MIX_WIDTH = A_WIDTH + B_WIDTH
D_FF = 4 * D_MODEL
EPS = 1e-5
NEG = -1e30

V7X_VMEM_LIMIT_BYTES = 62 * 1024 * 1024
MXU_COLS = 256
FFN_NORM_ROWS = 128
MIX_PIECES_PER_CHUNK = A_GROUPS + 1 + B_KV_HEADS * (PAIRS_PER_KV + 2) + 1

BF16 = jnp.bfloat16
F32 = jnp.float32


def _rms_scale(xf):
    return lax.rsqrt(jnp.mean(xf * xf, axis=-1, keepdims=True) + EPS)


def _params(n_axes):
    return pltpu.CompilerParams(
        dimension_semantics=("arbitrary",) * n_axes,
        vmem_limit_bytes=V7X_VMEM_LIMIT_BYTES)


def _const_spec(shape):
    return pl.BlockSpec(shape, lambda *_: (0,) * len(shape), pipeline_mode=pl.Buffered(1))


def _bias_kernel(table_ref, bucket_ref, o_ref):
    h = pl.program_id(0)
    bucket = bucket_ref[...]
    val = jnp.full(bucket.shape, NEG, F32)
    for b in range(N_BUCKETS):
        val = jnp.where(bucket == b, table_ref[b, h], val)
    col = lax.broadcasted_iota(jnp.int32, bucket.shape, 1)
    o_ref[0, 0] = val
    o_ref[1, 0] = jnp.where(col < CHUNK, NEG, val)


def _attention_bias(rel_bias_table, bucket):
    return pl.pallas_call(
        _bias_kernel,
        out_shape=jax.ShapeDtypeStruct(
            (2, B_KV_HEADS, PAIRS_PER_KV * CHUNK, 4 * CHUNK), F32),
        grid=(B_HEADS,),
        in_specs=[
            pl.BlockSpec(memory_space=pltpu.SMEM),
            pl.BlockSpec((CHUNK, 2 * CHUNK), lambda h: (0, 0)),
        ],
        out_specs=pl.BlockSpec(
            (2, 1, CHUNK, 2 * CHUNK),
            lambda h: (0, h // Q_PER_KV, (h % Q_PER_KV) // 2, h % 2)),
        compiler_params=_params(1),
        name="attn_bias",
    )(rel_bias_table, bucket)


def _half_swapped_pair(t):
    lane = lax.broadcasted_iota(jnp.int32, t.shape, 1)
    lo = lane < HEAD_DIM
    tr = pltpu.roll(t, HEAD_DIM, axis=1)
    zero = jnp.zeros_like(t)
    kv0 = jnp.concatenate([jnp.where(lo, t, zero), jnp.where(lo, zero, tr)], axis=0)
    kv1 = jnp.concatenate([jnp.where(lo, tr, zero), jnp.where(lo, zero, t)], axis=0)
    return kv0, kv1


def _mix_block_kernel(n_tiles, chunks_per_seq,
                      x1_ref, x3_ref, g_ref, win_ref, lng_ref, lnb_ref, ws_ref, bs_ref,
                      sinks_ref, bias_ref, ga_ref, gb_ref, wout_ref, wup_ref, wdn_ref,
                      o_ref, wup_bf_ref, wdn_bf_ref,
                      ws_bf, n_buf, u_nx, v_nx, q_nx, kv_nx, u_cu, v_cu, q_cu, kvbuf,
                      mix_nx, mix_cu, a_buf, b_buf):
    tq = x1_ref.shape[0]
    n_chunks = tq // CHUNK
    step = pl.program_id(0)

    def cast_ffn_weights():
        wup_bf_ref[...] = wup_ref[...].astype(BF16)
        wdn_bf_ref[...] = wdn_ref[...].astype(BF16)

    def project_pieces():
        xf = x1_ref[...]
        n_buf[...] = (xf * _rms_scale(xf) * g_ref[...]).astype(BF16)
        col = 0
        for dst in (u_nx, v_nx, q_nx, kv_nx):
            for c0 in range(0, dst.shape[1], MXU_COLS):
                def piece(dst=dst, c0=c0, col=col):
                    dst[:, c0:c0 + MXU_COLS] = jnp.dot(
                        n_buf[...], win_ref[:, col + c0:col + c0 + MXU_COLS],
                        preferred_element_type=F32).astype(BF16)
                yield piece
            col += dst.shape[1]

    def take_projection():
        kvbuf[0:CHUNK, :] = kvbuf[tq:tq + CHUNK, :]
        kvbuf[CHUNK:CHUNK + tq, :] = kv_nx[...]
        u_cu[...] = u_nx[...]
        v_cu[...] = v_nx[...]
        q_cu[...] = q_nx[...]

    def mix_a(c):
        rows = slice(c * CHUNK, (c + 1) * CHUNK)
        ss_a = jnp.zeros((CHUNK, A_GROUP_DIM), F32)
        for g in range(A_GROUPS):
            cols = slice(g * A_GROUP_DIM, (g + 1) * A_GROUP_DIM)
            ug = jax.nn.gelu(u_cu[rows, cols].astype(F32))
            vg = jax.nn.gelu(v_cu[rows, cols].astype(F32))
            mu = jnp.mean(vg, axis=-1, keepdims=True)
            vc = vg - mu
            var = jnp.mean(vc * vc, axis=-1, keepdims=True)
            vn = vc * lax.rsqrt(var + EPS) * lng_ref[:, cols] + lnb_ref[:, cols]
            mixed = jnp.dot(ws_bf[g], vn.astype(BF16), preferred_element_type=F32) + bs_ref[g]
            ag = ug * mixed
            a_buf[c, :, cols] = ag
            ss_a = ss_a + ag * ag
            yield
        scale_a = lax.rsqrt(jnp.sum(ss_a, axis=-1, keepdims=True) * (1.0 / A_WIDTH) + EPS)
        mix_nx[rows, 0:A_WIDTH] = (a_buf[c] * scale_a * ga_ref[...]).astype(BF16)
        yield

    def mix_b(c):
        rows = slice(c * CHUNK, (c + 1) * CHUNK)
        band = kvbuf[c * CHUNK:(c + 2) * CHUNK, :]
        k_stacks = _half_swapped_pair(band[:, :KV_WIDTH].astype(F32) * (HEAD_DIM ** -0.5))
        v_stacks = _half_swapped_pair(band[:, KV_WIDTH:].astype(F32))
        chunk_id = (step - 1) * n_chunks + c
        first = (chunk_id % chunks_per_seq == 0).astype(jnp.int32)
        lane = lax.broadcasted_iota(jnp.int32, (CHUNK, 2 * HEAD_DIM), 1)
        lo = lane < HEAD_DIM
        ss_b = jnp.zeros((CHUNK, 2 * HEAD_DIM), F32)
        for g in range(B_KV_HEADS):
            q_stack = jnp.concatenate(
                [q_cu[rows, (PAIRS_PER_KV * g + p) * 128:(PAIRS_PER_KV * g + p + 1) * 128]
                 for p in range(PAIRS_PER_KV)], axis=0)
            s_all = lax.dot_general(q_stack, k_stacks[g].astype(BF16),
                                    (((1,), (1,)), ((), ())),
                                    preferred_element_type=F32)
            yield
            p_rows = []
            rdens = []
            for p in range(PAIRS_PER_KV):
                p_cols = []
                rden_pair = []
                for par in range(2):
                    head = Q_PER_KV * g + 2 * p + par
                    rs = slice(p * CHUNK, (p + 1) * CHUNK)
                    cs = slice(par * 2 * CHUNK, (par + 1) * 2 * CHUNK)
                    s = s_all[rs, cs] + bias_ref[first, g, rs, cs]
                    sink = sinks_ref[head]
                    m = jnp.maximum(jnp.max(s, axis=-1, keepdims=True), sink)
                    e = jnp.exp(s - m)
                    den = jnp.sum(e, axis=-1, keepdims=True) + jnp.exp(sink - m)
                    p_cols.append(e.astype(BF16))
                    rden_pair.append(1.0 / den)
                p_rows.append(jnp.concatenate(p_cols, axis=1))
                rdens.append(rden_pair)
                yield
            probs = jnp.concatenate(p_rows, axis=0)
            o_all = jnp.dot(probs, v_stacks[g].astype(BF16), preferred_element_type=F32)
            for p in range(PAIRS_PER_KV):
                blk = PAIRS_PER_KV * g + p
                o = o_all[p * CHUNK:(p + 1) * CHUNK] * jnp.where(lo, rdens[p][0], rdens[p][1])
                b_buf[c, :, blk * 128:(blk + 1) * 128] = o
                ss_b = ss_b + o * o
            yield
        scale_b = lax.rsqrt(jnp.sum(ss_b, axis=-1, keepdims=True) * (1.0 / B_WIDTH) + EPS)
        mix_nx[rows, A_WIDTH:MIX_WIDTH] = (b_buf[c] * scale_b * gb_ref[...]).astype(BF16)
        yield

    def mix_pieces():
        for c in range(n_chunks):
            gens = [mix_b(c), mix_a(c)]
            while gens:
                for gen in list(gens):
                    try:
                        next(gen)
                        yield
                    except StopIteration:
                        gens.remove(gen)

    def output_pieces():
        mix_cu[...] = mix_nx[...]
        for c0 in range(0, D_MODEL, MXU_COLS):
            def piece(c0=c0):
                o_ref[:, c0:c0 + MXU_COLS] = x3_ref[:, c0:c0 + MXU_COLS] + jnp.dot(
                    mix_cu[...], wout_ref[:, c0:c0 + MXU_COLS], preferred_element_type=F32)
            yield piece

    def run(stage1, stage2, stage3):
        matmuls = []
        if stage3:
            matmuls += list(output_pieces())
        if stage2:
            take_projection()
        if stage1:
            matmuls += list(project_pieces())
        matmuls[0]()
        cast_ffn_weights()
        if not stage2:
            for piece in matmuls[1:]:
                piece()
            return
        n_yields = n_chunks * MIX_PIECES_PER_CHUNK
        done = 1
        for k, _ in enumerate(mix_pieces(), start=1):
            want = 1 + (k * (len(matmuls) - 1)) // n_yields
            while done < want:
                matmuls[done]()
                done += 1
        assert done == len(matmuls), (done, len(matmuls))

    @pl.when(step == 0)
    def _():
        row = lax.broadcasted_iota(jnp.int32, (CHUNK, CHUNK), 0)
        col = lax.broadcasted_iota(jnp.int32, (CHUNK, CHUNK), 1)
        for g in range(A_GROUPS):
            ws_bf[g] = jnp.where(row >= col, ws_ref[g], 0.0).astype(BF16)
        kvbuf[...] = jnp.zeros_like(kvbuf)
        run(True, False, False)

    @pl.when(step == 1)
    def _():
        run(True, True, False)

    @pl.when((step >= 2) & (step < n_tiles))
    def _():
        run(True, True, True)

    @pl.when(step == n_tiles)
    def _():
        run(False, True, True)

    @pl.when(step == n_tiles + 1)
    def _():
        run(False, False, True)


def _mix_block(x2, g, w_in, lng, lnb, w_s, b_s, sinks, bias, ga, gb, w_out, w_up, w_down,
               layer, tq, seq):
    t = x2.shape[0]
    n_tiles = t // tq
    n_chunks = tq // CHUNK
    proj = lambda w: pltpu.VMEM((tq, w), BF16)
    out_tile = lambda s: (jnp.clip(s - 2, 0, n_tiles - 1), 0)
    up_rows, dn_rows = D_MODEL // n_tiles, D_FF // n_tiles
    slab = lambda s: (jnp.minimum(s, n_tiles - 1), 0)
    return pl.pallas_call(
        functools.partial(_mix_block_kernel, n_tiles, seq // CHUNK),
        out_shape=[jax.ShapeDtypeStruct((t, D_MODEL), F32),
                   jax.ShapeDtypeStruct((D_MODEL, D_FF), BF16),
                   jax.ShapeDtypeStruct((D_FF, D_MODEL), BF16)],
        grid=(n_tiles + 2,),
        in_specs=[
            pl.BlockSpec((tq, D_MODEL), lambda s: (jnp.minimum(s, n_tiles - 1), 0)),
            pl.BlockSpec((tq, D_MODEL), out_tile),
            _const_spec((1, D_MODEL)),
            _const_spec(w_in.shape),
            _const_spec((1, A_WIDTH)),
            _const_spec((1, A_WIDTH)),
            _const_spec(w_s.shape),
            _const_spec(b_s.shape),
            pl.BlockSpec(memory_space=pltpu.SMEM),
            _const_spec(bias.shape),
            _const_spec((1, A_WIDTH)),
            _const_spec((1, B_WIDTH)),
            _const_spec(w_out.shape),
            pl.BlockSpec((None, up_rows, D_FF), lambda s: (layer,) + slab(s)),
            pl.BlockSpec((None, dn_rows, D_MODEL), lambda s: (layer,) + slab(s)),
        ],
        out_specs=[pl.BlockSpec((tq, D_MODEL), out_tile),
                   pl.BlockSpec((up_rows, D_FF), slab),
                   pl.BlockSpec((dn_rows, D_MODEL), slab)],
        scratch_shapes=[
            pltpu.VMEM((A_GROUPS, CHUNK, CHUNK), BF16),
            proj(D_MODEL),
            proj(A_WIDTH), proj(A_WIDTH), proj(B_WIDTH), proj(2 * KV_WIDTH),
            proj(A_WIDTH), proj(A_WIDTH), proj(B_WIDTH),
            pltpu.VMEM((tq + CHUNK, 2 * KV_WIDTH), BF16),
            proj(MIX_WIDTH), proj(MIX_WIDTH),
            pltpu.VMEM((n_chunks, CHUNK, A_WIDTH), F32),
            pltpu.VMEM((n_chunks, CHUNK, B_WIDTH), F32),
        ],
        compiler_params=_params(1),
        name="mix_block",
    )(x2, x2, g, w_in, lng, lnb, w_s, b_s, sinks, bias, ga, gb, w_out, w_up, w_down)


def _ffn_kernel(final_norm, h_hbm, g_ref, wu_ref, wd_ref, gf_ref, o_hbm,
                acc, nbuf, sem_in, sem_out):
    i = pl.program_id(0)
    f = pl.program_id(1)
    n_tiles = pl.num_programs(0)
    last_f = pl.num_programs(1) - 1
    tm = acc.shape[1]
    n_row_chunks = tm // FFN_NORM_ROWS
    slot = i % 2
    other = 1 - slot

    def h_copy(tile, s):
        return pltpu.make_async_copy(h_hbm.at[pl.ds(tile * tm, tm)], acc.at[s], sem_in)

    def o_copy(tile, s):
        return pltpu.make_async_copy(acc.at[s], o_hbm.at[pl.ds(tile * tm, tm)], sem_out.at[s])

    def chunk(k):
        return pl.ds(pl.multiple_of(k * FFN_NORM_ROWS, FFN_NORM_ROWS), FFN_NORM_ROWS)

    @pl.when(f == 0)
    def _():
        @pl.when(i == 0)
        def _():
            h_copy(0, 0).start()
        h_copy(i, slot).wait()

        def norm_rows(k, carry):
            hf = acc[slot, chunk(k), :]
            nbuf[chunk(k), :] = (hf * _rms_scale(hf) * g_ref[...]).astype(BF16)
            return carry
        lax.fori_loop(0, n_row_chunks, norm_rows, 0)

    @pl.when((f == 1) & (i + 1 < n_tiles))
    def _():
        @pl.when(i >= 1)
        def _():
            o_copy(i - 1, other).wait()
        h_copy(i + 1, other).start()

    z = jnp.maximum(jnp.dot(nbuf[...], wu_ref[...], preferred_element_type=F32), 0.0)
    update = jnp.dot((z * z).astype(BF16), wd_ref[...], preferred_element_type=F32)
    acc[slot] = acc[slot] + update

    @pl.when(f == last_f)
    def _():
        if final_norm:
            def final_rows(k, carry):
                y = acc[slot, chunk(k), :]
                acc[slot, chunk(k), :] = y * _rms_scale(y) * gf_ref[...]
                return carry
            lax.fori_loop(0, n_row_chunks, final_rows, 0)
        o_copy(i, slot).start()

        @pl.when(i == n_tiles - 1)
        def _():
            @pl.when(i >= 1)
            def _():
                o_copy(i - 1, other).wait()
            o_copy(i, slot).wait()


def _ffn(h, g, w_up, w_down, gf, final_norm, tm, tf):
    t = h.shape[0]
    assert D_FF // tf >= 2 and tm % FFN_NORM_ROWS == 0
    return pl.pallas_call(
        functools.partial(_ffn_kernel, final_norm),
        out_shape=jax.ShapeDtypeStruct((t, D_MODEL), F32),
        grid=(t // tm, D_FF // tf),
        in_specs=[
            pl.BlockSpec(memory_space=pl.ANY),
            _const_spec((1, D_MODEL)),
            pl.BlockSpec((D_MODEL, tf), lambda i, f: (0, f)),
            pl.BlockSpec((tf, D_MODEL), lambda i, f: (f, 0)),
            _const_spec((1, D_MODEL)),
        ],
        out_specs=pl.BlockSpec(memory_space=pl.ANY),
        scratch_shapes=[
            pltpu.VMEM((2, tm, D_MODEL), F32),
            pltpu.VMEM((tm, D_MODEL), BF16),
            pltpu.SemaphoreType.DMA(()),
            pltpu.SemaphoreType.DMA((2,)),
        ],
        compiler_params=_params(2),
        name="ffn",
    )(h, g, w_up, w_down, gf)


def _relative_buckets():
    i = jnp.arange(CHUNK)[:, None]
    j = jnp.arange(2 * CHUNK)[None, :]
    rel = i + CHUNK - j
    relc = jnp.maximum(rel, 0)
    n_exact = N_BUCKETS // 2
    relf = jnp.maximum(relc, n_exact).astype(F32)
    large = n_exact + (jnp.log(relf / n_exact) / math.log(MAX_DISTANCE / n_exact)
                       * (N_BUCKETS - n_exact)).astype(jnp.int32)
    large = jnp.minimum(large, N_BUCKETS - 1)
    bucket = jnp.where(relc < n_exact, relc, large)
    return jnp.where((rel >= 0) & (rel < CHUNK), bucket, -1).astype(jnp.int32)


def kernel(x, rel_bias_table, mix_norm_g, w_in, gate_norm_g, gate_norm_b, w_spatial, b_spatial,
           attn_sinks, out_norm_a_g, out_norm_b_g, w_out, ffn_norm_g, w_up, w_down, final_norm_g):
    bsz, seq, d = x.shape
    depth = w_in.shape[0]
    t = bsz * seq
    bias = _attention_bias(rel_bias_table.astype(F32), _relative_buckets())
    h = x.reshape(t, d)
    for layer in range(depth):
        b_s = jnp.broadcast_to(b_spatial[layer][:, :, None], (A_GROUPS, CHUNK, A_GROUP_DIM))
        h_mid, w_up_bf, w_down_bf = _mix_block(
            h, mix_norm_g[layer].reshape(1, d), w_in[layer].astype(BF16),
            gate_norm_g[layer].reshape(1, A_WIDTH), gate_norm_b[layer].reshape(1, A_WIDTH),
            w_spatial[layer], b_s, attn_sinks[layer], bias,
            out_norm_a_g[layer].reshape(1, A_WIDTH), out_norm_b_g[layer].reshape(1, B_WIDTH),
            w_out[layer].astype(BF16), w_up, w_down, layer=layer, tq=256, seq=seq)
        h = _ffn(h_mid, ffn_norm_g[layer].reshape(1, d), w_up_bf, w_down_bf,
                 final_norm_g.reshape(1, d), final_norm=layer == depth - 1, tm=1024, tf=2048)
    return h.reshape(bsz, seq, d)
```

```python
import functools
import math

import jax
import jax.numpy as jnp
from jax import lax
from jax.experimental import pallas as pl
from jax.experimental.pallas import tpu as pltpu

D_MODEL = 2048
CHUNK = 128
A_GROUPS = 8
A_GROUP_DIM = 128
A_WIDTH = A_GROUPS * A_GROUP_DIM
HEAD_DIM = 64
B_HEADS = 16
B_KV_HEADS = 2
Q_PER_KV = B_HEADS // B_KV_HEADS
PAIRS_PER_KV = Q_PER_KV // 2
B_WIDTH = B_HEADS * HEAD_DIM
KV_WIDTH = B_KV_HEADS * HEAD_DIM
N_BUCKETS = 32
MAX_DISTANCE = 128
MIX_WIDTH = A_WIDTH + B_WIDTH
D_FF = 4 * D_MODEL
EPS = 1e-5
NEG = -1e30
LOG2E = 1.4426950408889634
GELU_TANH_SCALE = math.sqrt(2.0 / math.pi)
GELU_CUBIC = 0.044715

V7X_VMEM_LIMIT_BYTES = 62 * 1024 * 1024
MXU_COLS = 256
FFN_NORM_ROWS = 128
MIX_PIECES_PER_CHUNK = A_GROUPS + 1 + B_KV_HEADS * (PAIRS_PER_KV + 2) + 1

BF16 = jnp.bfloat16
F32 = jnp.float32


def _rms_scale(xf):
    return lax.rsqrt(jnp.mean(xf * xf, axis=-1, keepdims=True) + EPS)


def _gelu(x):
    a = -2.0 * GELU_TANH_SCALE * LOG2E
    return x / (1.0 + jnp.exp2(x * (x * x * (a * GELU_CUBIC) + a)))


def _params(n_axes):
    return pltpu.CompilerParams(
        dimension_semantics=("arbitrary",) * n_axes,
        vmem_limit_bytes=V7X_VMEM_LIMIT_BYTES)


def _const_spec(shape):
    return pl.BlockSpec(shape, lambda *_: (0,) * len(shape), pipeline_mode=pl.Buffered(1))


def _bias_kernel(table_ref, bucket_ref, o_ref):
    bucket = bucket_ref[...]
    col = lax.broadcasted_iota(jnp.int32, bucket.shape, 1)

    def one_kv_head(g, carry):
        for p in range(PAIRS_PER_KV):
            for par in range(2):
                h = g * Q_PER_KV + 2 * p + par
                val = jnp.full(bucket.shape, NEG, F32)
                for b in range(N_BUCKETS):
                    val = jnp.where(bucket == b, table_ref[b, h] * LOG2E, val)
                rs = slice(p * CHUNK, (p + 1) * CHUNK)
                cs = slice(par * 2 * CHUNK, (par + 1) * 2 * CHUNK)
                o_ref[0, g, rs, cs] = val
                o_ref[1, g, rs, cs] = jnp.where(col < CHUNK, NEG, val)
        return carry
    lax.fori_loop(0, B_KV_HEADS, one_kv_head, 0)


def _attention_bias(rel_bias_table, bucket):
    shape = (2, B_KV_HEADS, PAIRS_PER_KV * CHUNK, 4 * CHUNK)
    return pl.pallas_call(
        _bias_kernel,
        out_shape=jax.ShapeDtypeStruct(shape, F32),
        grid=(1,),
        in_specs=[
            pl.BlockSpec(memory_space=pltpu.SMEM),
            pl.BlockSpec((CHUNK, 2 * CHUNK), lambda s: (0, 0)),
        ],
        out_specs=pl.BlockSpec(shape, lambda s: (0, 0, 0, 0)),
        compiler_params=_params(1),
        name="attn_bias",
    )(rel_bias_table, bucket)


def _half_swapped_pair(t):
    lane = lax.broadcasted_iota(jnp.int32, t.shape, 1)
    lo = lane < HEAD_DIM
    tr = pltpu.roll(t, HEAD_DIM, axis=1)
    zero = jnp.zeros_like(t)
    kv0 = jnp.concatenate([jnp.where(lo, t, zero), jnp.where(lo, zero, tr)], axis=0)
    kv1 = jnp.concatenate([jnp.where(lo, tr, zero), jnp.where(lo, zero, t)], axis=0)
    return kv0, kv1


def _mix_block_kernel(n_tiles, chunks_per_seq,
                      x1_ref, x3_ref, g_ref, win_ref, lng_ref, lnb_ref, ws_ref, bs_ref,
                      sinks_ref, bias_ref, ga_ref, gb_ref, wout_ref, wup_ref, wdn_ref,
                      o_ref, wup_bf_ref, wdn_bf_ref,
                      ws_bf, n_buf, u_nx, v_nx, q_nx, kv_nx, u_cu, v_cu, q_cu, kvbuf,
                      mix_nx, mix_cu, a_buf, b_buf):
    tq = x1_ref.shape[0]
    n_chunks = tq // CHUNK
    step = pl.program_id(0)

    def cast_ffn_weights():
        wup_bf_ref[...] = wup_ref[...].astype(BF16)
        wdn_bf_ref[...] = wdn_ref[...].astype(BF16)

    def project_pieces():
        xf = x1_ref[...]
        n_buf[...] = (xf * _rms_scale(xf) * g_ref[...]).astype(BF16)
        col = 0
        for dst in (u_nx, v_nx, q_nx, kv_nx):
            for c0 in range(0, dst.shape[1], MXU_COLS):
                def piece(dst=dst, c0=c0, col=col):
                    dst[:, c0:c0 + MXU_COLS] = jnp.dot(
                        n_buf[...], win_ref[:, col + c0:col + c0 + MXU_COLS],
                        preferred_element_type=F32).astype(BF16)
                yield piece
            col += dst.shape[1]

    def take_projection():
        kvbuf[0:CHUNK, :] = kvbuf[tq:tq + CHUNK, :]
        kvbuf[CHUNK:CHUNK + tq, :] = kv_nx[...]
        u_cu[...] = u_nx[...]
        v_cu[...] = v_nx[...]
        q_cu[...] = q_nx[...]

    def mix_a(c):
        rows = slice(c * CHUNK, (c + 1) * CHUNK)
        ss_a = jnp.zeros((CHUNK, A_GROUP_DIM), F32)
        for g in range(A_GROUPS):
            cols = slice(g * A_GROUP_DIM, (g + 1) * A_GROUP_DIM)
            ug = _gelu(u_cu[rows, cols].astype(F32))
            vg = _gelu(v_cu[rows, cols].astype(F32))
            mu = jnp.mean(vg, axis=-1, keepdims=True)
            vc = vg - mu
            var = jnp.mean(vc * vc, axis=-1, keepdims=True)
            vn = vc * lax.rsqrt(var + EPS) * lng_ref[:, cols] + lnb_ref[:, cols]
            mixed = jnp.dot(ws_bf[g], vn.astype(BF16), preferred_element_type=F32) + bs_ref[g]
            ag = ug * mixed
            a_buf[c, :, cols] = ag
            ss_a = ss_a + ag * ag
            yield
        scale_a = lax.rsqrt(jnp.sum(ss_a, axis=-1, keepdims=True) * (1.0 / A_WIDTH) + EPS)
        mix_nx[rows, 0:A_WIDTH] = (a_buf[c] * scale_a * ga_ref[...]).astype(BF16)
        yield

    def mix_b(c):
        rows = slice(c * CHUNK, (c + 1) * CHUNK)
        band = kvbuf[c * CHUNK:(c + 2) * CHUNK, :]
        k_stacks = _half_swapped_pair(band[:, :KV_WIDTH].astype(F32) * (HEAD_DIM ** -0.5 * LOG2E))
        v_stacks = _half_swapped_pair(band[:, KV_WIDTH:].astype(F32))
        chunk_id = (step - 1) * n_chunks + c
        first = (chunk_id % chunks_per_seq == 0).astype(jnp.int32)
        lane = lax.broadcasted_iota(jnp.int32, (CHUNK, 2 * HEAD_DIM), 1)
        lo = lane < HEAD_DIM
        srow = lax.broadcasted_iota(jnp.int32, (4 * CHUNK, 2 * HEAD_DIM), 0)
        slane = lax.broadcasted_iota(jnp.int32, (4 * CHUNK, 2 * HEAD_DIM), 1)
        row_sum_cols = ((srow < 2 * CHUNK) == (slane < HEAD_DIM)).astype(BF16)
        ss_b = jnp.zeros((CHUNK, 2 * HEAD_DIM), F32)
        for g in range(B_KV_HEADS):
            q_stack = jnp.concatenate(
                [q_cu[rows, (PAIRS_PER_KV * g + p) * 128:(PAIRS_PER_KV * g + p + 1) * 128]
                 for p in range(PAIRS_PER_KV)], axis=0)
            s_all = lax.dot_general(q_stack, k_stacks[g].astype(BF16),
                                    (((1,), (1,)), ((), ())),
                                    preferred_element_type=F32)
            yield
            p_rows = []
            sink_terms = []
            for p in range(PAIRS_PER_KV):
                p_cols = []
                sink_pair = []
                for par in range(2):
                    head = Q_PER_KV * g + 2 * p + par
                    rs = slice(p * CHUNK, (p + 1) * CHUNK)
                    cs = slice(par * 2 * CHUNK, (par + 1) * 2 * CHUNK)
                    s = s_all[rs, cs] + bias_ref[first, g, rs, cs]
                    sink = sinks_ref[head] * LOG2E
                    m = jnp.maximum(jnp.max(s, axis=-1, keepdims=True), sink)
                    p_cols.append(jnp.exp2(s - m).astype(BF16))
                    sink_pair.append(jnp.exp2(sink - m))
                p_rows.append(jnp.concatenate(p_cols, axis=1))
                sink_terms.append(sink_pair)
                yield
            probs = jnp.concatenate(p_rows, axis=0)
            values = jnp.concatenate([v_stacks[g].astype(BF16), row_sum_cols], axis=1)
            o_all = jnp.dot(probs, values, preferred_element_type=F32)
            for p in range(PAIRS_PER_KV):
                blk = PAIRS_PER_KV * g + p
                o_p = o_all[p * CHUNK:(p + 1) * CHUNK]
                den = o_p[:, 2 * HEAD_DIM:] + jnp.where(lo, sink_terms[p][0], sink_terms[p][1])
                o = o_p[:, :2 * HEAD_DIM] / den
                b_buf[c, :, blk * 128:(blk + 1) * 128] = o
                ss_b = ss_b + o * o
            yield
        scale_b = lax.rsqrt(jnp.sum(ss_b, axis=-1, keepdims=True) * (1.0 / B_WIDTH) + EPS)
        mix_nx[rows, A_WIDTH:MIX_WIDTH] = (b_buf[c] * scale_b * gb_ref[...]).astype(BF16)
        yield

    def mix_pieces():
        for c in range(n_chunks):
            gens = [mix_b(c), mix_a(c)]
            while gens:
                for gen in list(gens):
                    try:
                        next(gen)
                        yield
                    except StopIteration:
                        gens.remove(gen)

    def output_pieces():
        mix_cu[...] = mix_nx[...]
        for c0 in range(0, D_MODEL, MXU_COLS):
            def piece(c0=c0):
                o_ref[:, c0:c0 + MXU_COLS] = x3_ref[:, c0:c0 + MXU_COLS] + jnp.dot(
                    mix_cu[...], wout_ref[:, c0:c0 + MXU_COLS], preferred_element_type=F32)
            yield piece

    def run(stage1, stage2, stage3):
        matmuls = []
        if stage3:
            matmuls += list(output_pieces())
        if stage2:
            take_projection()
        if stage1:
            matmuls += list(project_pieces())
        matmuls[0]()
        cast_ffn_weights()
        if not stage2:
            for piece in matmuls[1:]:
                piece()
            return
        n_yields = n_chunks * MIX_PIECES_PER_CHUNK
        done = 1
        for k, _ in enumerate(mix_pieces(), start=1):
            want = 1 + (k * (len(matmuls) - 1)) // n_yields
            while done < want:
                matmuls[done]()
                done += 1
        assert done == len(matmuls), (done, len(matmuls))

    @pl.when(step == 0)
    def _():
        row = lax.broadcasted_iota(jnp.int32, (CHUNK, CHUNK), 0)
        col = lax.broadcasted_iota(jnp.int32, (CHUNK, CHUNK), 1)
        for g in range(A_GROUPS):
            ws_bf[g] = jnp.where(row >= col, ws_ref[g], 0.0).astype(BF16)
        kvbuf[...] = jnp.zeros_like(kvbuf)
        run(True, False, False)

    @pl.when(step == 1)
    def _():
        run(True, True, False)

    @pl.when((step >= 2) & (step < n_tiles))
    def _():
        run(True, True, True)

    @pl.when(step == n_tiles)
    def _():
        run(False, True, True)

    @pl.when(step == n_tiles + 1)
    def _():
        run(False, False, True)


def _mix_block(x2, g, w_in, lng, lnb, w_s, b_s, sinks, bias, ga, gb, w_out, w_up, w_down,
               layer, tq, seq):
    t = x2.shape[0]
    n_tiles = t // tq
    n_chunks = tq // CHUNK
    proj = lambda w: pltpu.VMEM((tq, w), BF16)
    out_tile = lambda s: (jnp.clip(s - 2, 0, n_tiles - 1), 0)
    up_rows, dn_rows = D_MODEL // n_tiles, D_FF // n_tiles
    slab = lambda s: (jnp.minimum(s, n_tiles - 1), 0)
    return pl.pallas_call(
        functools.partial(_mix_block_kernel, n_tiles, seq // CHUNK),
        out_shape=[jax.ShapeDtypeStruct((t, D_MODEL), F32),
                   jax.ShapeDtypeStruct((D_MODEL, D_FF), BF16),
                   jax.ShapeDtypeStruct((D_FF, D_MODEL), BF16)],
        grid=(n_tiles + 2,),
        in_specs=[
            pl.BlockSpec((tq, D_MODEL), lambda s: (jnp.minimum(s, n_tiles - 1), 0)),
            pl.BlockSpec((tq, D_MODEL), out_tile),
            _const_spec((1, D_MODEL)),
            _const_spec(w_in.shape),
            _const_spec((1, A_WIDTH)),
            _const_spec((1, A_WIDTH)),
            _const_spec(w_s.shape),
            _const_spec(b_s.shape),
            pl.BlockSpec(memory_space=pltpu.SMEM),
            _const_spec(bias.shape),
            _const_spec((1, A_WIDTH)),
            _const_spec((1, B_WIDTH)),
            _const_spec(w_out.shape),
            pl.BlockSpec((None, up_rows, D_FF), lambda s: (layer,) + slab(s)),
            pl.BlockSpec((None, dn_rows, D_MODEL), lambda s: (layer,) + slab(s)),
        ],
        out_specs=[pl.BlockSpec((tq, D_MODEL), out_tile),
                   pl.BlockSpec((up_rows, D_FF), slab),
                   pl.BlockSpec((dn_rows, D_MODEL), slab)],
        scratch_shapes=[
            pltpu.VMEM((A_GROUPS, CHUNK, CHUNK), BF16),
            proj(D_MODEL),
            proj(A_WIDTH), proj(A_WIDTH), proj(B_WIDTH), proj(2 * KV_WIDTH),
            proj(A_WIDTH), proj(A_WIDTH), proj(B_WIDTH),
            pltpu.VMEM((tq + CHUNK, 2 * KV_WIDTH), BF16),
            proj(MIX_WIDTH), proj(MIX_WIDTH),
            pltpu.VMEM((n_chunks, CHUNK, A_WIDTH), F32),
            pltpu.VMEM((n_chunks, CHUNK, B_WIDTH), F32),
        ],
        compiler_params=_params(1),
        name="mix_block",
    )(x2, x2, g, w_in, lng, lnb, w_s, b_s, sinks, bias, ga, gb, w_out, w_up, w_down)


def _ffn_kernel(final_norm, h_hbm, g_ref, wu_ref, wd_ref, gf_ref, o_hbm,
                acc, nbuf, sem_in, sem_out):
    i = pl.program_id(0)
    f = pl.program_id(1)
    n_tiles = pl.num_programs(0)
    last_f = pl.num_programs(1) - 1
    tm = acc.shape[1]
    n_row_chunks = tm // FFN_NORM_ROWS
    slot = i % 2
    other = 1 - slot

    def h_copy(tile, s):
        return pltpu.make_async_copy(h_hbm.at[pl.ds(tile * tm, tm)], acc.at[s], sem_in)

    def o_copy(tile, s):
        return pltpu.make_async_copy(acc.at[s], o_hbm.at[pl.ds(tile * tm, tm)], sem_out.at[s])

    def chunk(k):
        return pl.ds(pl.multiple_of(k * FFN_NORM_ROWS, FFN_NORM_ROWS), FFN_NORM_ROWS)

    @pl.when(f == 0)
    def _():
        @pl.when(i == 0)
        def _():
            h_copy(0, 0).start()
        h_copy(i, slot).wait()

        def norm_rows(k, carry):
            hf = acc[slot, chunk(k), :]
            nbuf[chunk(k), :] = (hf * _rms_scale(hf) * g_ref[...]).astype(BF16)
            return carry
        lax.fori_loop(0, n_row_chunks, norm_rows, 0, unroll=True)

    @pl.when((f == 1) & (i + 1 < n_tiles))
    def _():
        @pl.when(i >= 1)
        def _():
            o_copy(i - 1, other).wait()
        h_copy(i + 1, other).start()

    z = jnp.maximum(jnp.dot(nbuf[...], wu_ref[...], preferred_element_type=F32), 0.0)
    update = jnp.dot((z * z).astype(BF16), wd_ref[...], preferred_element_type=F32)
    acc[slot] = acc[slot] + update

    @pl.when(f == last_f)
    def _():
        if final_norm:
            def final_rows(k, carry):
                y = acc[slot, chunk(k), :]
                acc[slot, chunk(k), :] = y * _rms_scale(y) * gf_ref[...]
                return carry
            lax.fori_loop(0, n_row_chunks, final_rows, 0, unroll=True)
        o_copy(i, slot).start()

        @pl.when(i == n_tiles - 1)
        def _():
            @pl.when(i >= 1)
            def _():
                o_copy(i - 1, other).wait()
            o_copy(i, slot).wait()


def _ffn(h, g, w_up, w_down, gf, final_norm, tm, tf):
    t = h.shape[0]
    assert D_FF // tf >= 2 and tm % FFN_NORM_ROWS == 0
    return pl.pallas_call(
        functools.partial(_ffn_kernel, final_norm),
        out_shape=jax.ShapeDtypeStruct((t, D_MODEL), F32),
        grid=(t // tm, D_FF // tf),
        in_specs=[
            pl.BlockSpec(memory_space=pl.ANY),
            _const_spec((1, D_MODEL)),
            pl.BlockSpec((D_MODEL, tf), lambda i, f: (0, f)),
            pl.BlockSpec((tf, D_MODEL), lambda i, f: (f, 0)),
            _const_spec((1, D_MODEL)),
        ],
        out_specs=pl.BlockSpec(memory_space=pl.ANY),
        scratch_shapes=[
            pltpu.VMEM((2, tm, D_MODEL), F32),
            pltpu.VMEM((tm, D_MODEL), BF16),
            pltpu.SemaphoreType.DMA(()),
            pltpu.SemaphoreType.DMA((2,)),
        ],
        compiler_params=_params(2),
        name="ffn",
    )(h, g, w_up, w_down, gf)


def _relative_buckets():
    i = jnp.arange(CHUNK)[:, None]
    j = jnp.arange(2 * CHUNK)[None, :]
    rel = i + CHUNK - j
    relc = jnp.maximum(rel, 0)
    n_exact = N_BUCKETS // 2
    relf = jnp.maximum(relc, n_exact).astype(F32)
    large = n_exact + (jnp.log(relf / n_exact) / math.log(MAX_DISTANCE / n_exact)
                       * (N_BUCKETS - n_exact)).astype(jnp.int32)
    large = jnp.minimum(large, N_BUCKETS - 1)
    bucket = jnp.where(relc < n_exact, relc, large)
    return jnp.where((rel >= 0) & (rel < CHUNK), bucket, -1).astype(jnp.int32)


def kernel(x, rel_bias_table, mix_norm_g, w_in, gate_norm_g, gate_norm_b, w_spatial, b_spatial,
           attn_sinks, out_norm_a_g, out_norm_b_g, w_out, ffn_norm_g, w_up, w_down, final_norm_g):
    bsz, seq, d = x.shape
    depth = w_in.shape[0]
    t = bsz * seq
    bias = _attention_bias(rel_bias_table.astype(F32), _relative_buckets())
    h = x.reshape(t, d)
    for layer in range(depth):
        b_s = jnp.broadcast_to(b_spatial[layer][:, :, None], (A_GROUPS, CHUNK, A_GROUP_DIM))
        h_mid, w_up_bf, w_down_bf = _mix_block(
            h, mix_norm_g[layer].reshape(1, d), w_in[layer].astype(BF16),
            gate_norm_g[layer].reshape(1, A_WIDTH), gate_norm_b[layer].reshape(1, A_WIDTH),
            w_spatial[layer], b_s, attn_sinks[layer], bias,
            out_norm_a_g[layer].reshape(1, A_WIDTH), out_norm_b_g[layer].reshape(1, B_WIDTH),
            w_out[layer].astype(BF16), w_up, w_down, layer=layer, tq=256, seq=seq)
        h = _ffn(h_mid, ffn_norm_g[layer].reshape(1, d), w_up_bf, w_down_bf,
                 final_norm_g.reshape(1, d), final_norm=layer == depth - 1, tm=1024, tf=2048)
    return h.reshape(bsz, seq, d)
```

```python
import functools
import math

import jax
import jax.numpy as jnp
from jax import lax
from jax.experimental import pallas as pl
from jax.experimental.pallas import tpu as pltpu

D_MODEL = 2048
CHUNK = 128
A_GROUPS = 8
A_GROUP_DIM = 128
A_WIDTH = A_GROUPS * A_GROUP_DIM
HEAD_DIM = 64
B_HEADS = 16
B_KV_HEADS = 2
Q_PER_KV = B_HEADS // B_KV_HEADS
PAIRS_PER_KV = Q_PER_KV // 2
B_WIDTH = B_HEADS * HEAD_DIM
KV_WIDTH = B_KV_HEADS * HEAD_DIM
N_BUCKETS = 32
MAX_DISTANCE = 128
MIX_WIDTH = A_WIDTH + B_WIDTH
D_FF = 4 * D_MODEL
EPS = 1e-5
NEG = -1e30
LOG2E = 1.4426950408889634
GELU_TANH_SCALE = math.sqrt(2.0 / math.pi)
GELU_CUBIC = 0.044715

V7X_VMEM_LIMIT_BYTES = 62 * 1024 * 1024
MXU_COLS = 256
FFN_NORM_ROWS = 128
MIX_PIECES_PER_CHUNK = A_GROUPS + 1 + B_KV_HEADS * (PAIRS_PER_KV + 2) + 1

BF16 = jnp.bfloat16
F32 = jnp.float32


def _rms_scale(xf):
    return lax.rsqrt(jnp.mean(xf * xf, axis=-1, keepdims=True) + EPS)


def _gelu(x):
    a = -2.0 * GELU_TANH_SCALE * LOG2E
    return x / (1.0 + jnp.exp2(x * (x * x * (a * GELU_CUBIC) + a)))


def _params(n_axes):
    return pltpu.CompilerParams(
        dimension_semantics=("arbitrary",) * n_axes,
        vmem_limit_bytes=V7X_VMEM_LIMIT_BYTES)


def _const_spec(shape):
    return pl.BlockSpec(shape, lambda *_: (0,) * len(shape), pipeline_mode=pl.Buffered(1))


def _bias_kernel(table_ref, bucket_ref, o_ref):
    bucket = bucket_ref[...]
    col = lax.broadcasted_iota(jnp.int32, bucket.shape, 1)

    def one_kv_head(g, carry):
        for p in range(PAIRS_PER_KV):
            for par in range(2):
                h = g * Q_PER_KV + 2 * p + par
                val = jnp.full(bucket.shape, NEG, F32)
                for b in range(N_BUCKETS):
                    val = jnp.where(bucket == b, table_ref[b, h] * LOG2E, val)
                rs = slice(p * CHUNK, (p + 1) * CHUNK)
                cs = slice(par * 2 * CHUNK, (par + 1) * 2 * CHUNK)
                o_ref[0, g, rs, cs] = val
                o_ref[1, g, rs, cs] = jnp.where(col < CHUNK, NEG, val)
        return carry
    lax.fori_loop(0, B_KV_HEADS, one_kv_head, 0)


def _attention_bias(rel_bias_table, bucket):
    shape = (2, B_KV_HEADS, PAIRS_PER_KV * CHUNK, 4 * CHUNK)
    return pl.pallas_call(
        _bias_kernel,
        out_shape=jax.ShapeDtypeStruct(shape, F32),
        grid=(1,),
        in_specs=[
            pl.BlockSpec(memory_space=pltpu.SMEM),
            pl.BlockSpec((CHUNK, 2 * CHUNK), lambda s: (0, 0)),
        ],
        out_specs=pl.BlockSpec(shape, lambda s: (0, 0, 0, 0)),
        compiler_params=_params(1),
        name="attn_bias",
    )(rel_bias_table, bucket)


def _half_swapped_pair(t):
    lane = lax.broadcasted_iota(jnp.int32, t.shape, 1)
    lo = lane < HEAD_DIM
    tr = pltpu.roll(t, HEAD_DIM, axis=1)
    zero = jnp.zeros_like(t)
    kv0 = jnp.concatenate([jnp.where(lo, t, zero), jnp.where(lo, zero, tr)], axis=0)
    kv1 = jnp.concatenate([jnp.where(lo, tr, zero), jnp.where(lo, zero, t)], axis=0)
    return kv0, kv1


def _mix_block_kernel(n_tiles, chunks_per_seq,
                      x1_ref, x3_ref, g_ref, win_ref, lng_ref, lnb_ref, ws_ref, bs_ref,
                      sinks_ref, bias_ref, ga_ref, gb_ref, wout_ref, wup_ref, wdn_ref,
                      o_ref, wup_bf_ref, wdn_bf_ref,
                      ws_bf, n_buf, u_nx, v_nx, q_nx, kv_nx, u_cu, v_cu, q_cu, kvbuf,
                      mix_nx, mix_cu, a_buf, b_buf):
    tq = x1_ref.shape[0]
    n_chunks = tq // CHUNK
    step = pl.program_id(0)

    def cast_ffn_weights():
        wup_bf_ref[...] = wup_ref[...].astype(BF16)
        wdn_bf_ref[...] = wdn_ref[...].astype(BF16)

    def project_pieces():
        xf = x1_ref[...]
        n_buf[...] = (xf * _rms_scale(xf) * g_ref[...]).astype(BF16)
        col = 0
        for dst in (u_nx, v_nx, q_nx, kv_nx):
            for c0 in range(0, dst.shape[1], MXU_COLS):
                def piece(dst=dst, c0=c0, col=col):
                    dst[:, c0:c0 + MXU_COLS] = jnp.dot(
                        n_buf[...], win_ref[:, col + c0:col + c0 + MXU_COLS],
                        preferred_element_type=F32).astype(BF16)
                yield piece
            col += dst.shape[1]

    def take_projection():
        kvbuf[0:CHUNK, :] = kvbuf[tq:tq + CHUNK, :]
        kvbuf[CHUNK:CHUNK + tq, :] = kv_nx[...]
        u_cu[...] = u_nx[...]
        v_cu[...] = v_nx[...]
        q_cu[...] = q_nx[...]

    def mix_a(c):
        rows = slice(c * CHUNK, (c + 1) * CHUNK)
        ss_a = jnp.zeros((CHUNK, A_GROUP_DIM), F32)
        for g in range(A_GROUPS):
            cols = slice(g * A_GROUP_DIM, (g + 1) * A_GROUP_DIM)
            ug = _gelu(u_cu[rows, cols].astype(F32))
            vg = _gelu(v_cu[rows, cols].astype(F32))
            mu = jnp.mean(vg, axis=-1, keepdims=True)
            vc = vg - mu
            var = jnp.mean(vc * vc, axis=-1, keepdims=True)
            vn = vc * lax.rsqrt(var + EPS) * lng_ref[:, cols] + lnb_ref[:, cols]
            mixed = jnp.dot(ws_bf[g], vn.astype(BF16), preferred_element_type=F32) + bs_ref[g]
            ag = ug * mixed
            a_buf[c, :, cols] = ag
            ss_a = ss_a + ag * ag
            yield
        scale_a = lax.rsqrt(jnp.sum(ss_a, axis=-1, keepdims=True) * (1.0 / A_WIDTH) + EPS)
        mix_nx[rows, 0:A_WIDTH] = (a_buf[c] * scale_a * ga_ref[...]).astype(BF16)
        yield

    def mix_b(c):
        rows = slice(c * CHUNK, (c + 1) * CHUNK)
        band = kvbuf[c * CHUNK:(c + 2) * CHUNK, :]
        k_stacks = _half_swapped_pair(band[:, :KV_WIDTH].astype(F32) * (HEAD_DIM ** -0.5 * LOG2E))
        v_stacks = _half_swapped_pair(band[:, KV_WIDTH:].astype(F32))
        chunk_id = (step - 1) * n_chunks + c
        first = (chunk_id % chunks_per_seq == 0).astype(jnp.int32)
        lane = lax.broadcasted_iota(jnp.int32, (CHUNK, 2 * HEAD_DIM), 1)
        lo = lane < HEAD_DIM
        srow = lax.broadcasted_iota(jnp.int32, (4 * CHUNK, 2 * HEAD_DIM), 0)
        slane = lax.broadcasted_iota(jnp.int32, (4 * CHUNK, 2 * HEAD_DIM), 1)
        row_sum_cols = ((srow < 2 * CHUNK) == (slane < HEAD_DIM)).astype(BF16)
        ss_b = jnp.zeros((CHUNK, 2 * HEAD_DIM), F32)
        for g in range(B_KV_HEADS):
            q_stack = jnp.concatenate(
                [q_cu[rows, (PAIRS_PER_KV * g + p) * 128:(PAIRS_PER_KV * g + p + 1) * 128]
                 for p in range(PAIRS_PER_KV)], axis=0)
            s_all = lax.dot_general(q_stack, k_stacks[g].astype(BF16),
                                    (((1,), (1,)), ((), ())),
                                    preferred_element_type=F32)
            yield
            p_rows = []
            sink_terms = []
            for p in range(PAIRS_PER_KV):
                p_cols = []
                sink_pair = []
                for par in range(2):
                    head = Q_PER_KV * g + 2 * p + par
                    rs = slice(p * CHUNK, (p + 1) * CHUNK)
                    cs = slice(par * 2 * CHUNK, (par + 1) * 2 * CHUNK)
                    s = s_all[rs, cs] + bias_ref[first, g, rs, cs]
                    sink = sinks_ref[head] * LOG2E
                    m = jnp.maximum(jnp.max(s, axis=-1, keepdims=True), sink)
                    p_cols.append(jnp.exp2(s - m).astype(BF16))
                    sink_pair.append(jnp.exp2(sink - m))
                p_rows.append(jnp.concatenate(p_cols, axis=1))
                sink_terms.append(sink_pair)
                yield
            probs = jnp.concatenate(p_rows, axis=0)
            values = jnp.concatenate([v_stacks[g].astype(BF16), row_sum_cols], axis=1)
            o_all = jnp.dot(probs, values, preferred_element_type=F32)
            for p in range(PAIRS_PER_KV):
                blk = PAIRS_PER_KV * g + p
                o_p = o_all[p * CHUNK:(p + 1) * CHUNK]
                den = o_p[:, 2 * HEAD_DIM:] + jnp.where(lo, sink_terms[p][0], sink_terms[p][1])
                o = o_p[:, :2 * HEAD_DIM] / den
                b_buf[c, :, blk * 128:(blk + 1) * 128] = o
                ss_b = ss_b + o * o
            yield
        scale_b = lax.rsqrt(jnp.sum(ss_b, axis=-1, keepdims=True) * (1.0 / B_WIDTH) + EPS)
        mix_nx[rows, A_WIDTH:MIX_WIDTH] = (b_buf[c] * scale_b * gb_ref[...]).astype(BF16)
        yield

    def mix_pieces():
        for c in range(n_chunks):
            gens = [mix_b(c), mix_a(c)]
            while gens:
                for gen in list(gens):
                    try:
                        next(gen)
                        yield
                    except StopIteration:
                        gens.remove(gen)

    def output_pieces():
        mix_cu[...] = mix_nx[...]
        for c0 in range(0, D_MODEL, MXU_COLS):
            def piece(c0=c0):
                o_ref[:, c0:c0 + MXU_COLS] = x3_ref[:, c0:c0 + MXU_COLS] + jnp.dot(
                    mix_cu[...], wout_ref[:, c0:c0 + MXU_COLS], preferred_element_type=F32)
            yield piece

    def run():
        matmuls = list(output_pieces())
        take_projection()
        matmuls += list(project_pieces())
        matmuls[0]()
        cast_ffn_weights()
        n_yields = n_chunks * MIX_PIECES_PER_CHUNK
        done = 1
        for k, _ in enumerate(mix_pieces(), start=1):
            want = 1 + (k * (len(matmuls) - 1)) // n_yields
            while done < want:
                matmuls[done]()
                done += 1
        assert done == len(matmuls), (done, len(matmuls))

    @pl.when(step == 0)
    def _():
        row = lax.broadcasted_iota(jnp.int32, (CHUNK, CHUNK), 0)
        col = lax.broadcasted_iota(jnp.int32, (CHUNK, CHUNK), 1)
        for g in range(A_GROUPS):
            ws_bf[g] = jnp.where(row >= col, ws_ref[g], 0.0).astype(BF16)
        for ref in (kvbuf, u_nx, v_nx, q_nx, kv_nx, mix_nx):
            ref[...] = jnp.zeros_like(ref)

    run()


def _mix_block(x2, g, w_in, lng, lnb, w_s, b_s, sinks, bias, ga, gb, w_out, w_up, w_down,
               layer, tq, seq):
    t = x2.shape[0]
    n_tiles = t // tq
    n_chunks = tq // CHUNK
    proj = lambda w: pltpu.VMEM((tq, w), BF16)
    out_tile = lambda s: (jnp.clip(s - 2, 0, n_tiles - 1), 0)
    up_rows, dn_rows = D_MODEL // n_tiles, D_FF // n_tiles
    slab = lambda s: (jnp.minimum(s, n_tiles - 1), 0)
    return pl.pallas_call(
        functools.partial(_mix_block_kernel, n_tiles, seq // CHUNK),
        out_shape=[jax.ShapeDtypeStruct((t, D_MODEL), F32),
                   jax.ShapeDtypeStruct((D_MODEL, D_FF), BF16),
                   jax.ShapeDtypeStruct((D_FF, D_MODEL), BF16)],
        grid=(n_tiles + 2,),
        in_specs=[
            pl.BlockSpec((tq, D_MODEL), lambda s: (jnp.minimum(s, n_tiles - 1), 0)),
            pl.BlockSpec((tq, D_MODEL), out_tile),
            _const_spec((1, D_MODEL)),
            _const_spec(w_in.shape),
            _const_spec((1, A_WIDTH)),
            _const_spec((1, A_WIDTH)),
            _const_spec(w_s.shape),
            _const_spec(b_s.shape),
            pl.BlockSpec(memory_space=pltpu.SMEM),
            _const_spec(bias.shape),
            _const_spec((1, A_WIDTH)),
            _const_spec((1, B_WIDTH)),
            _const_spec(w_out.shape),
            pl.BlockSpec((None, up_rows, D_FF), lambda s: (layer,) + slab(s)),
            pl.BlockSpec((None, dn_rows, D_MODEL), lambda s: (layer,) + slab(s)),
        ],
        out_specs=[pl.BlockSpec((tq, D_MODEL), out_tile),
                   pl.BlockSpec((up_rows, D_FF), slab),
                   pl.BlockSpec((dn_rows, D_MODEL), slab)],
        scratch_shapes=[
            pltpu.VMEM((A_GROUPS, CHUNK, CHUNK), BF16),
            proj(D_MODEL),
            proj(A_WIDTH), proj(A_WIDTH), proj(B_WIDTH), proj(2 * KV_WIDTH),
            proj(A_WIDTH), proj(A_WIDTH), proj(B_WIDTH),
            pltpu.VMEM((tq + CHUNK, 2 * KV_WIDTH), BF16),
            proj(MIX_WIDTH), proj(MIX_WIDTH),
            pltpu.VMEM((n_chunks, CHUNK, A_WIDTH), F32),
            pltpu.VMEM((n_chunks, CHUNK, B_WIDTH), F32),
        ],
        compiler_params=_params(1),
        name="mix_block",
    )(x2, x2, g, w_in, lng, lnb, w_s, b_s, sinks, bias, ga, gb, w_out, w_up, w_down)


def _ffn_kernel(final_norm, h_hbm, g_ref, wu_ref, wd_ref, gf_ref, o_hbm,
                acc, nbuf, sem_in, sem_out):
    i = pl.program_id(0)
    f = pl.program_id(1)
    n_tiles = pl.num_programs(0)
    last_f = pl.num_programs(1) - 1
    tm = acc.shape[1]
    n_row_chunks = tm // FFN_NORM_ROWS
    slot = i % 2
    other = 1 - slot

    def h_copy(tile, s):
        return pltpu.make_async_copy(h_hbm.at[pl.ds(tile * tm, tm)], acc.at[s], sem_in)

    def o_copy(tile, s):
        return pltpu.make_async_copy(acc.at[s], o_hbm.at[pl.ds(tile * tm, tm)], sem_out.at[s])

    def chunk(k):
        return pl.ds(pl.multiple_of(k * FFN_NORM_ROWS, FFN_NORM_ROWS), FFN_NORM_ROWS)

    @pl.when(f == 0)
    def _():
        @pl.when(i == 0)
        def _():
            h_copy(0, 0).start()
        h_copy(i, slot).wait()

        def norm_rows(k, carry):
            hf = acc[slot, chunk(k), :]
            nbuf[chunk(k), :] = (hf * _rms_scale(hf) * g_ref[...]).astype(BF16)
            return carry
        lax.fori_loop(0, n_row_chunks, norm_rows, 0, unroll=True)

    @pl.when((f == 1) & (i + 1 < n_tiles))
    def _():
        @pl.when(i >= 1)
        def _():
            o_copy(i - 1, other).wait()
        h_copy(i + 1, other).start()

    z = jnp.maximum(jnp.dot(nbuf[...], wu_ref[...], preferred_element_type=F32), 0.0)
    update = jnp.dot((z * z).astype(BF16), wd_ref[...], preferred_element_type=F32)
    acc[slot] = acc[slot] + update

    @pl.when(f == last_f)
    def _():
        if final_norm:
            def final_rows(k, carry):
                y = acc[slot, chunk(k), :]
                acc[slot, chunk(k), :] = y * _rms_scale(y) * gf_ref[...]
                return carry
            lax.fori_loop(0, n_row_chunks, final_rows, 0, unroll=True)
        o_copy(i, slot).start()

        @pl.when(i == n_tiles - 1)
        def _():
            @pl.when(i >= 1)
            def _():
                o_copy(i - 1, other).wait()
            o_copy(i, slot).wait()


def _ffn(h, g, w_up, w_down, gf, final_norm, tm, tf):
    t = h.shape[0]
    assert D_FF // tf >= 2 and tm % FFN_NORM_ROWS == 0
    return pl.pallas_call(
        functools.partial(_ffn_kernel, final_norm),
        out_shape=jax.ShapeDtypeStruct((t, D_MODEL), F32),
        grid=(t // tm, D_FF // tf),
        in_specs=[
            pl.BlockSpec(memory_space=pl.ANY),
            _const_spec((1, D_MODEL)),
            pl.BlockSpec((D_MODEL, tf), lambda i, f: (0, f)),
            pl.BlockSpec((tf, D_MODEL), lambda i, f: (f, 0)),
            _const_spec((1, D_MODEL)),
        ],
        out_specs=pl.BlockSpec(memory_space=pl.ANY),
        scratch_shapes=[
            pltpu.VMEM((2, tm, D_MODEL), F32),
            pltpu.VMEM((tm, D_MODEL), BF16),
            pltpu.SemaphoreType.DMA(()),
            pltpu.SemaphoreType.DMA((2,)),
        ],
        compiler_params=_params(2),
        name="ffn",
    )(h, g, w_up, w_down, gf)


def _relative_buckets():
    i = jnp.arange(CHUNK)[:, None]
    j = jnp.arange(2 * CHUNK)[None, :]
    rel = i + CHUNK - j
    relc = jnp.maximum(rel, 0)
    n_exact = N_BUCKETS // 2
    relf = jnp.maximum(relc, n_exact).astype(F32)
    large = n_exact + (jnp.log(relf / n_exact) / math.log(MAX_DISTANCE / n_exact)
                       * (N_BUCKETS - n_exact)).astype(jnp.int32)
    large = jnp.minimum(large, N_BUCKETS - 1)
    bucket = jnp.where(relc < n_exact, relc, large)
    return jnp.where((rel >= 0) & (rel < CHUNK), bucket, -1).astype(jnp.int32)


def kernel(x, rel_bias_table, mix_norm_g, w_in, gate_norm_g, gate_norm_b, w_spatial, b_spatial,
           attn_sinks, out_norm_a_g, out_norm_b_g, w_out, ffn_norm_g, w_up, w_down, final_norm_g):
    bsz, seq, d = x.shape
    depth = w_in.shape[0]
    t = bsz * seq
    bias = _attention_bias(rel_bias_table.astype(F32), _relative_buckets())
    h = x.reshape(t, d)
    for layer in range(depth):
        b_s = jnp.broadcast_to(b_spatial[layer][:, :, None], (A_GROUPS, CHUNK, A_GROUP_DIM))
        h_mid, w_up_bf, w_down_bf = _mix_block(
            h, mix_norm_g[layer].reshape(1, d), w_in[layer].astype(BF16),
            gate_norm_g[layer].reshape(1, A_WIDTH), gate_norm_b[layer].reshape(1, A_WIDTH),
            w_spatial[layer], b_s, attn_sinks[layer], bias,
            out_norm_a_g[layer].reshape(1, A_WIDTH), out_norm_b_g[layer].reshape(1, B_WIDTH),
            w_out[layer].astype(BF16), w_up, w_down, layer=layer, tq=256, seq=seq)
        h = _ffn(h_mid, ffn_norm_g[layer].reshape(1, d), w_up_bf, w_down_bf,
                 final_norm_g.reshape(1, d), final_norm=layer == depth - 1, tm=1024, tf=2048)
    return h.reshape(bsz, seq, d)
```

```python
import functools
import math

import jax
import jax.numpy as jnp
from jax import lax
from jax.experimental import pallas as pl
from jax.experimental.pallas import tpu as pltpu

D_MODEL = 2048
CHUNK = 128
A_GROUPS = 8
A_GROUP_DIM = 128
A_WIDTH = A_GROUPS * A_GROUP_DIM
HEAD_DIM = 64
B_HEADS = 16
B_KV_HEADS = 2
Q_PER_KV = B_HEADS // B_KV_HEADS
PAIRS_PER_KV = Q_PER_KV // 2
B_WIDTH = B_HEADS * HEAD_DIM
KV_WIDTH = B_KV_HEADS * HEAD_DIM
N_BUCKETS = 32
MAX_DISTANCE = 128
MIX_WIDTH = A_WIDTH + B_WIDTH
D_FF = 4 * D_MODEL
EPS = 1e-5
NEG = -1e30
LOG2E = 1.4426950408889634
GELU_TANH_SCALE = math.sqrt(2.0 / math.pi)
GELU_CUBIC = 0.044715

V7X_VMEM_LIMIT_BYTES = 62 * 1024 * 1024
MXU_COLS = 256
FFN_NORM_ROWS = 128
TRAILING_MATMULS = 2
MIX_PIECES_PER_CHUNK = A_GROUPS + 1 + B_KV_HEADS * (PAIRS_PER_KV + 2) + 1

BF16 = jnp.bfloat16
F32 = jnp.float32


def _rms_scale(xf):
    return lax.rsqrt(jnp.mean(xf * xf, axis=-1, keepdims=True) + EPS)


def _gelu(x):
    a = -2.0 * GELU_TANH_SCALE * LOG2E
    return x / (1.0 + jnp.exp2(x * (x * x * (a * GELU_CUBIC) + a)))


def _params(n_axes):
    return pltpu.CompilerParams(
        dimension_semantics=("arbitrary",) * n_axes,
        vmem_limit_bytes=V7X_VMEM_LIMIT_BYTES)


def _const_spec(shape):
    return pl.BlockSpec(shape, lambda *_: (0,) * len(shape), pipeline_mode=pl.Buffered(1))


def _bias_kernel(table_ref, bucket_ref, o_ref):
    bucket = bucket_ref[...]
    col = lax.broadcasted_iota(jnp.int32, bucket.shape, 1)

    def one_kv_head(g, carry):
        for p in range(PAIRS_PER_KV):
            for par in range(2):
                h = g * Q_PER_KV + 2 * p + par
                val = jnp.full(bucket.shape, NEG, F32)
                for b in range(N_BUCKETS):
                    val = jnp.where(bucket == b, table_ref[b, h] * LOG2E, val)
                rs = slice(p * CHUNK, (p + 1) * CHUNK)
                cs = slice(par * 2 * CHUNK, (par + 1) * 2 * CHUNK)
                o_ref[0, g, rs, cs] = val
                o_ref[1, g, rs, cs] = jnp.where(col < CHUNK, NEG, val)
        return carry
    lax.fori_loop(0, B_KV_HEADS, one_kv_head, 0)


def _attention_bias(rel_bias_table, bucket):
    shape = (2, B_KV_HEADS, PAIRS_PER_KV * CHUNK, 4 * CHUNK)
    return pl.pallas_call(
        _bias_kernel,
        out_shape=jax.ShapeDtypeStruct(shape, F32),
        grid=(1,),
        in_specs=[
            pl.BlockSpec(memory_space=pltpu.SMEM),
            pl.BlockSpec((CHUNK, 2 * CHUNK), lambda s: (0, 0)),
        ],
        out_specs=pl.BlockSpec(shape, lambda s: (0, 0, 0, 0)),
        compiler_params=_params(1),
        name="attn_bias",
    )(rel_bias_table, bucket)


def _half_swapped_pair(t):
    lane = lax.broadcasted_iota(jnp.int32, t.shape, 1)
    lo = lane < HEAD_DIM
    tr = pltpu.roll(t, HEAD_DIM, axis=1)
    zero = jnp.zeros_like(t)
    kv0 = jnp.concatenate([jnp.where(lo, t, zero), jnp.where(lo, zero, tr)], axis=0)
    kv1 = jnp.concatenate([jnp.where(lo, tr, zero), jnp.where(lo, zero, t)], axis=0)
    return kv0, kv1


def _mix_block_kernel(n_tiles, chunks_per_seq,
                      x1_ref, x3_ref, g_ref, win_ref, lng_ref, lnb_ref, ws_ref, bs_ref,
                      sinks_ref, bias_ref, ga_ref, gb_ref, wout_ref, wup_ref, wdn_ref,
                      o_ref, wup_bf_ref, wdn_bf_ref,
                      ws_bf, n_buf, u_nx, v_nx, q_nx, kv_nx, u_cu, v_cu, q_cu, kvbuf,
                      mix_nx, mix_cu, a_buf, b_buf):
    tq = x1_ref.shape[0]
    n_chunks = tq // CHUNK
    step = pl.program_id(0)

    def cast_ffn_weights():
        wup_bf_ref[...] = wup_ref[...].astype(BF16)
        wdn_bf_ref[...] = wdn_ref[...].astype(BF16)

    def project_pieces():
        xf = x1_ref[...]
        n_buf[...] = (xf * _rms_scale(xf) * g_ref[...]).astype(BF16)
        col = 0
        for dst in (u_nx, v_nx, q_nx, kv_nx):
            for c0 in range(0, dst.shape[1], MXU_COLS):
                def piece(dst=dst, c0=c0, col=col):
                    dst[:, c0:c0 + MXU_COLS] = jnp.dot(
                        n_buf[...], win_ref[:, col + c0:col + c0 + MXU_COLS],
                        preferred_element_type=F32).astype(BF16)
                yield piece
            col += dst.shape[1]

    def take_projection():
        kvbuf[0:CHUNK, :] = kvbuf[tq:tq + CHUNK, :]
        kvbuf[CHUNK:CHUNK + tq, :] = kv_nx[...]
        u_cu[...] = u_nx[...]
        v_cu[...] = v_nx[...]
        q_cu[...] = q_nx[...]

    def mix_a(c):
        rows = slice(c * CHUNK, (c + 1) * CHUNK)
        ss_a = jnp.zeros((CHUNK, A_GROUP_DIM), F32)
        for g in range(A_GROUPS):
            cols = slice(g * A_GROUP_DIM, (g + 1) * A_GROUP_DIM)
            ug = _gelu(u_cu[rows, cols].astype(F32))
            vg = _gelu(v_cu[rows, cols].astype(F32))
            mu = jnp.mean(vg, axis=-1, keepdims=True)
            vc = vg - mu
            var = jnp.mean(vc * vc, axis=-1, keepdims=True)
            vn = vc * lax.rsqrt(var + EPS) * lng_ref[:, cols] + lnb_ref[:, cols]
            mixed = jnp.dot(ws_bf[g], vn.astype(BF16), preferred_element_type=F32) + bs_ref[g]
            ag = ug * mixed
            a_buf[c, :, cols] = ag
            ss_a = ss_a + ag * ag
            yield
        scale_a = lax.rsqrt(jnp.sum(ss_a, axis=-1, keepdims=True) * (1.0 / A_WIDTH) + EPS)
        mix_nx[rows, 0:A_WIDTH] = (a_buf[c] * scale_a * ga_ref[...]).astype(BF16)
        yield

    def mix_b(c):
        rows = slice(c * CHUNK, (c + 1) * CHUNK)
        band = kvbuf[c * CHUNK:(c + 2) * CHUNK, :]
        k_stacks = _half_swapped_pair(band[:, :KV_WIDTH].astype(F32) * (HEAD_DIM ** -0.5 * LOG2E))
        v_stacks = _half_swapped_pair(band[:, KV_WIDTH:].astype(F32))
        chunk_id = (step - 1) * n_chunks + c
        first = (chunk_id % chunks_per_seq == 0).astype(jnp.int32)
        lane = lax.broadcasted_iota(jnp.int32, (CHUNK, 2 * HEAD_DIM), 1)
        lo = lane < HEAD_DIM
        srow = lax.broadcasted_iota(jnp.int32, (4 * CHUNK, 2 * HEAD_DIM), 0)
        slane = lax.broadcasted_iota(jnp.int32, (4 * CHUNK, 2 * HEAD_DIM), 1)
        row_sum_cols = ((srow < 2 * CHUNK) == (slane < HEAD_DIM)).astype(BF16)
        ss_b = jnp.zeros((CHUNK, 2 * HEAD_DIM), F32)
        for g in range(B_KV_HEADS):
            q_stack = jnp.concatenate(
                [q_cu[rows, (PAIRS_PER_KV * g + p) * 128:(PAIRS_PER_KV * g + p + 1) * 128]
                 for p in range(PAIRS_PER_KV)], axis=0)
            s_all = lax.dot_general(q_stack, k_stacks[g].astype(BF16),
                                    (((1,), (1,)), ((), ())),
                                    preferred_element_type=F32)
            yield
            p_rows = []
            sink_terms = []
            for p in range(PAIRS_PER_KV):
                p_cols = []
                sink_pair = []
                for par in range(2):
                    head = Q_PER_KV * g + 2 * p + par
                    rs = slice(p * CHUNK, (p + 1) * CHUNK)
                    cs = slice(par * 2 * CHUNK, (par + 1) * 2 * CHUNK)
                    s = s_all[rs, cs] + bias_ref[first, g, rs, cs]
                    sink = sinks_ref[head] * LOG2E
                    m = jnp.maximum(jnp.max(s, axis=-1, keepdims=True), sink)
                    p_cols.append(jnp.exp2(s - m).astype(BF16))
                    sink_pair.append(jnp.exp2(sink - m))
                p_rows.append(jnp.concatenate(p_cols, axis=1))
                sink_terms.append(sink_pair)
                yield
            probs = jnp.concatenate(p_rows, axis=0)
            values = jnp.concatenate([v_stacks[g].astype(BF16), row_sum_cols], axis=1)
            o_all = jnp.dot(probs, values, preferred_element_type=F32)
            for p in range(PAIRS_PER_KV):
                blk = PAIRS_PER_KV * g + p
                o_p = o_all[p * CHUNK:(p + 1) * CHUNK]
                den = o_p[:, 2 * HEAD_DIM:] + jnp.where(lo, sink_terms[p][0], sink_terms[p][1])
                o = o_p[:, :2 * HEAD_DIM] / den
                b_buf[c, :, blk * 128:(blk + 1) * 128] = o
                ss_b = ss_b + o * o
            yield
        scale_b = lax.rsqrt(jnp.sum(ss_b, axis=-1, keepdims=True) * (1.0 / B_WIDTH) + EPS)
        mix_nx[rows, A_WIDTH:MIX_WIDTH] = (b_buf[c] * scale_b * gb_ref[...]).astype(BF16)
        yield

    def mix_pieces():
        for c in range(n_chunks):
            gens = [mix_b(c), mix_a(c)]
            while gens:
                for gen in list(gens):
                    try:
                        next(gen)
                        yield
                    except StopIteration:
                        gens.remove(gen)

    def output_pieces():
        mix_cu[...] = mix_nx[...]
        for c0 in range(0, D_MODEL, MXU_COLS):
            def piece(c0=c0):
                o_ref[:, c0:c0 + MXU_COLS] = x3_ref[:, c0:c0 + MXU_COLS] + jnp.dot(
                    mix_cu[...], wout_ref[:, c0:c0 + MXU_COLS], preferred_element_type=F32)
            yield piece

    def run():
        out_gen = output_pieces()
        next(out_gen)()
        matmuls = list(out_gen)
        take_projection()
        matmuls += list(project_pieces())
        cast_ffn_weights()
        n_yields = n_chunks * MIX_PIECES_PER_CHUNK
        spread = len(matmuls) - TRAILING_MATMULS
        done = 0
        for k, _ in enumerate(mix_pieces(), start=1):
            want = (k * spread) // n_yields
            while done < want:
                matmuls[done]()
                done += 1
        for piece in matmuls[done:]:
            piece()

    @pl.when(step == 0)
    def _():
        row = lax.broadcasted_iota(jnp.int32, (CHUNK, CHUNK), 0)
        col = lax.broadcasted_iota(jnp.int32, (CHUNK, CHUNK), 1)
        for g in range(A_GROUPS):
            ws_bf[g] = jnp.where(row >= col, ws_ref[g], 0.0).astype(BF16)
        for ref in (kvbuf, u_nx, v_nx, q_nx, kv_nx, mix_nx):
            ref[...] = jnp.zeros_like(ref)

    run()


def _mix_block(x2, g, w_in, lng, lnb, w_s, b_s, sinks, bias, ga, gb, w_out, w_up, w_down,
               layer, tq, seq):
    t = x2.shape[0]
    n_tiles = t // tq
    n_chunks = tq // CHUNK
    proj = lambda w: pltpu.VMEM((tq, w), BF16)
    out_tile = lambda s: (jnp.clip(s - 2, 0, n_tiles - 1), 0)
    up_rows, dn_rows = D_MODEL // n_tiles, D_FF // n_tiles
    slab = lambda s: (jnp.minimum(s, n_tiles - 1), 0)
    return pl.pallas_call(
        functools.partial(_mix_block_kernel, n_tiles, seq // CHUNK),
        out_shape=[jax.ShapeDtypeStruct((t, D_MODEL), F32),
                   jax.ShapeDtypeStruct((D_MODEL, D_FF), BF16),
                   jax.ShapeDtypeStruct((D_FF, D_MODEL), BF16)],
        grid=(n_tiles + 2,),
        in_specs=[
            pl.BlockSpec((tq, D_MODEL), lambda s: (jnp.minimum(s, n_tiles - 1), 0)),
            pl.BlockSpec((tq, D_MODEL), out_tile),
            _const_spec((1, D_MODEL)),
            _const_spec(w_in.shape),
            _const_spec((1, A_WIDTH)),
            _const_spec((1, A_WIDTH)),
            _const_spec(w_s.shape),
            _const_spec(b_s.shape),
            pl.BlockSpec(memory_space=pltpu.SMEM),
            _const_spec(bias.shape),
            _const_spec((1, A_WIDTH)),
            _const_spec((1, B_WIDTH)),
            _const_spec(w_out.shape),
            pl.BlockSpec((None, up_rows, D_FF), lambda s: (layer,) + slab(s)),
            pl.BlockSpec((None, dn_rows, D_MODEL), lambda s: (layer,) + slab(s)),
        ],
        out_specs=[pl.BlockSpec((tq, D_MODEL), out_tile),
                   pl.BlockSpec((up_rows, D_FF), slab),
                   pl.BlockSpec((dn_rows, D_MODEL), slab)],
        scratch_shapes=[
            pltpu.VMEM((A_GROUPS, CHUNK, CHUNK), BF16),
            proj(D_MODEL),
            proj(A_WIDTH), proj(A_WIDTH), proj(B_WIDTH), proj(2 * KV_WIDTH),
            proj(A_WIDTH), proj(A_WIDTH), proj(B_WIDTH),
            pltpu.VMEM((tq + CHUNK, 2 * KV_WIDTH), BF16),
            proj(MIX_WIDTH), proj(MIX_WIDTH),
            pltpu.VMEM((n_chunks, CHUNK, A_WIDTH), F32),
            pltpu.VMEM((n_chunks, CHUNK, B_WIDTH), F32),
        ],
        compiler_params=_params(1),
        name="mix_block",
    )(x2, x2, g, w_in, lng, lnb, w_s, b_s, sinks, bias, ga, gb, w_out, w_up, w_down)


def _ffn_kernel(final_norm, h_hbm, g_ref, wu_ref, wd_ref, gf_ref, o_hbm,
                acc, nbuf, sem_in, sem_out):
    i = pl.program_id(0)
    f = pl.program_id(1)
    n_tiles = pl.num_programs(0)
    last_f = pl.num_programs(1) - 1
    tm = acc.shape[1]
    n_row_chunks = tm // FFN_NORM_ROWS
    slot = i % 2
    other = 1 - slot

    def h_copy(tile, s):
        return pltpu.make_async_copy(h_hbm.at[pl.ds(tile * tm, tm)], acc.at[s], sem_in)

    def o_copy(tile, s):
        return pltpu.make_async_copy(acc.at[s], o_hbm.at[pl.ds(tile * tm, tm)], sem_out.at[s])

    def chunk(k):
        return pl.ds(pl.multiple_of(k * FFN_NORM_ROWS, FFN_NORM_ROWS), FFN_NORM_ROWS)

    @pl.when(f == 0)
    def _():
        @pl.when(i == 0)
        def _():
            h_copy(0, 0).start()
        h_copy(i, slot).wait()

        def norm_rows(k, carry):
            hf = acc[slot, chunk(k), :]
            nbuf[chunk(k), :] = (hf * _rms_scale(hf) * g_ref[...]).astype(BF16)
            return carry
        lax.fori_loop(0, n_row_chunks, norm_rows, 0, unroll=True)

    @pl.when((f == 1) & (i + 1 < n_tiles))
    def _():
        @pl.when(i >= 1)
        def _():
            o_copy(i - 1, other).wait()
        h_copy(i + 1, other).start()

    z = jnp.maximum(jnp.dot(nbuf[...], wu_ref[...], preferred_element_type=F32), 0.0)
    update = jnp.dot((z * z).astype(BF16), wd_ref[...], preferred_element_type=F32)
    acc[slot] = acc[slot] + update

    @pl.when(f == last_f)
    def _():
        if final_norm:
            def final_rows(k, carry):
                y = acc[slot, chunk(k), :]
                acc[slot, chunk(k), :] = y * _rms_scale(y) * gf_ref[...]
                return carry
            lax.fori_loop(0, n_row_chunks, final_rows, 0, unroll=True)
        o_copy(i, slot).start()

        @pl.when(i == n_tiles - 1)
        def _():
            @pl.when(i >= 1)
            def _():
                o_copy(i - 1, other).wait()
            o_copy(i, slot).wait()


def _ffn(h, g, w_up, w_down, gf, final_norm, tm, tf):
    t = h.shape[0]
    assert D_FF // tf >= 2 and tm % FFN_NORM_ROWS == 0
    return pl.pallas_call(
        functools.partial(_ffn_kernel, final_norm),
        out_shape=jax.ShapeDtypeStruct((t, D_MODEL), F32),
        grid=(t // tm, D_FF // tf),
        in_specs=[
            pl.BlockSpec(memory_space=pl.ANY),
            _const_spec((1, D_MODEL)),
            pl.BlockSpec((D_MODEL, tf), lambda i, f: (0, f)),
            pl.BlockSpec((tf, D_MODEL), lambda i, f: (f, 0)),
            _const_spec((1, D_MODEL)),
        ],
        out_specs=pl.BlockSpec(memory_space=pl.ANY),
        scratch_shapes=[
            pltpu.VMEM((2, tm, D_MODEL), F32),
            pltpu.VMEM((tm, D_MODEL), BF16),
            pltpu.SemaphoreType.DMA(()),
            pltpu.SemaphoreType.DMA((2,)),
        ],
        compiler_params=_params(2),
        name="ffn",
    )(h, g, w_up, w_down, gf)


def _relative_buckets():
    i = jnp.arange(CHUNK)[:, None]
    j = jnp.arange(2 * CHUNK)[None, :]
    rel = i + CHUNK - j
    relc = jnp.maximum(rel, 0)
    n_exact = N_BUCKETS // 2
    relf = jnp.maximum(relc, n_exact).astype(F32)
    large = n_exact + (jnp.log(relf / n_exact) / math.log(MAX_DISTANCE / n_exact)
                       * (N_BUCKETS - n_exact)).astype(jnp.int32)
    large = jnp.minimum(large, N_BUCKETS - 1)
    bucket = jnp.where(relc < n_exact, relc, large)
    return jnp.where((rel >= 0) & (rel < CHUNK), bucket, -1).astype(jnp.int32)


def kernel(x, rel_bias_table, mix_norm_g, w_in, gate_norm_g, gate_norm_b, w_spatial, b_spatial,
           attn_sinks, out_norm_a_g, out_norm_b_g, w_out, ffn_norm_g, w_up, w_down, final_norm_g):
    bsz, seq, d = x.shape
    depth = w_in.shape[0]
    t = bsz * seq
    bias = _attention_bias(rel_bias_table.astype(F32), _relative_buckets())
    h = x.reshape(t, d)
    for layer in range(depth):
        b_s = jnp.broadcast_to(b_spatial[layer][:, :, None], (A_GROUPS, CHUNK, A_GROUP_DIM))
        h_mid, w_up_bf, w_down_bf = _mix_block(
            h, mix_norm_g[layer].reshape(1, d), w_in[layer].astype(BF16),
            gate_norm_g[layer].reshape(1, A_WIDTH), gate_norm_b[layer].reshape(1, A_WIDTH),
            w_spatial[layer], b_s, attn_sinks[layer], bias,
            out_norm_a_g[layer].reshape(1, A_WIDTH), out_norm_b_g[layer].reshape(1, B_WIDTH),
            w_out[layer].astype(BF16), w_up, w_down, layer=layer, tq=256, seq=seq)
        h = _ffn(h_mid, ffn_norm_g[layer].reshape(1, d), w_up_bf, w_down_bf,
                 final_norm_g.reshape(1, d), final_norm=layer == depth - 1, tm=1024, tf=2048)
    return h.reshape(bsz, seq, d)
```

```python
import functools
import math

import jax
import jax.numpy as jnp
from jax import lax
from jax.experimental import pallas as pl
from jax.experimental.pallas import tpu as pltpu

D_MODEL = 2048
CHUNK = 128
A_GROUPS = 8
A_GROUP_DIM = 128
A_WIDTH = A_GROUPS * A_GROUP_DIM
HEAD_DIM = 64
B_HEADS = 16
B_KV_HEADS = 2
Q_PER_KV = B_HEADS // B_KV_HEADS
PAIRS_PER_KV = Q_PER_KV // 2
B_WIDTH = B_HEADS * HEAD_DIM
KV_WIDTH = B_KV_HEADS * HEAD_DIM
N_BUCKETS = 32
MAX_DISTANCE = 128
MIX_WIDTH = A_WIDTH + B_WIDTH
D_FF = 4 * D_MODEL
EPS = 1e-5
NEG = -1e30
LOG2E = 1.4426950408889634
GELU_TANH_SCALE = math.sqrt(2.0 / math.pi)
GELU_CUBIC = 0.044715

V7X_VMEM_LIMIT_BYTES = 62 * 1024 * 1024
MXU_COLS = 256
FFN_NORM_ROWS = 128
TRAILING_MATMULS = 2
WEIGHT_STAGE_ROWS = 128
MIX_PIECES_PER_CHUNK = A_GROUPS + 1 + B_KV_HEADS * (PAIRS_PER_KV + 2) + 1

BF16 = jnp.bfloat16
F32 = jnp.float32


def _rms_scale(xf):
    return lax.rsqrt(jnp.mean(xf * xf, axis=-1, keepdims=True) + EPS)


def _gelu(x):
    a = -2.0 * GELU_TANH_SCALE * LOG2E
    return x / (1.0 + jnp.exp2(x * (x * x * (a * GELU_CUBIC) + a)))


def _params(n_axes):
    return pltpu.CompilerParams(
        dimension_semantics=("arbitrary",) * n_axes,
        vmem_limit_bytes=V7X_VMEM_LIMIT_BYTES)


def _const_spec(shape):
    return pl.BlockSpec(shape, lambda *_: (0,) * len(shape), pipeline_mode=pl.Buffered(1))


def _bias_kernel(table_ref, bucket_ref, o_ref):
    bucket = bucket_ref[...]
    col = lax.broadcasted_iota(jnp.int32, bucket.shape, 1)

    def one_kv_head(g, carry):
        for p in range(PAIRS_PER_KV):
            for par in range(2):
                h = g * Q_PER_KV + 2 * p + par
                val = jnp.full(bucket.shape, NEG, F32)
                for b in range(N_BUCKETS):
                    val = jnp.where(bucket == b, table_ref[b, h] * LOG2E, val)
                rs = slice(p * CHUNK, (p + 1) * CHUNK)
                cs = slice(par * 2 * CHUNK, (par + 1) * 2 * CHUNK)
                o_ref[0, g, rs, cs] = val
                o_ref[1, g, rs, cs] = jnp.where(col < CHUNK, NEG, val)
        return carry
    lax.fori_loop(0, B_KV_HEADS, one_kv_head, 0)


def _attention_bias(rel_bias_table, bucket):
    shape = (2, B_KV_HEADS, PAIRS_PER_KV * CHUNK, 4 * CHUNK)
    return pl.pallas_call(
        _bias_kernel,
        out_shape=jax.ShapeDtypeStruct(shape, F32),
        grid=(1,),
        in_specs=[
            pl.BlockSpec(memory_space=pltpu.SMEM),
            pl.BlockSpec((CHUNK, 2 * CHUNK), lambda s: (0, 0)),
        ],
        out_specs=pl.BlockSpec(shape, lambda s: (0, 0, 0, 0)),
        compiler_params=_params(1),
        name="attn_bias",
    )(rel_bias_table, bucket)


def _half_swapped_pair(t):
    lane = lax.broadcasted_iota(jnp.int32, t.shape, 1)
    lo = lane < HEAD_DIM
    tr = pltpu.roll(t, HEAD_DIM, axis=1)
    zero = jnp.zeros_like(t)
    kv0 = jnp.concatenate([jnp.where(lo, t, zero), jnp.where(lo, zero, tr)], axis=0)
    kv1 = jnp.concatenate([jnp.where(lo, tr, zero), jnp.where(lo, zero, t)], axis=0)
    return kv0, kv1


def _mix_block_kernel(layer, chunks_per_seq,
                      x1_ref, x3_ref, g_ref, win_hbm, lng_ref, lnb_ref, ws_ref, bs_ref,
                      sinks_ref, bias_ref, ga_ref, gb_ref, wout_hbm, wup_ref, wdn_ref,
                      o_ref, wup_bf_ref, wdn_bf_ref,
                      ws_bf, win_ref, wout_ref, stage_in, stage_out, stage_sem,
                      n_buf, u_nx, v_nx, q_nx, kv_nx, u_cu, v_cu, q_cu, kvbuf,
                      mix_nx, mix_cu, a_buf, b_buf):
    tq = x1_ref.shape[0]
    n_chunks = tq // CHUNK
    step = pl.program_id(0)

    def load_projection_weights():
        rows = stage_in.shape[1]
        n_slabs = D_MODEL // rows

        def copies(k, slot):
            src = pl.ds(k * rows, rows)
            return (pltpu.make_async_copy(win_hbm.at[layer, src], stage_in.at[slot],
                                          stage_sem.at[0, slot]),
                    pltpu.make_async_copy(wout_hbm.at[layer, src], stage_out.at[slot],
                                          stage_sem.at[1, slot]))

        for cp in copies(0, 0):
            cp.start()

        def slab(k, carry):
            slot = k % 2

            @pl.when(k + 1 < n_slabs)
            def _():
                for cp in copies(k + 1, 1 - slot):
                    cp.start()
            for cp in copies(k, slot):
                cp.wait()
            dst = pl.ds(pl.multiple_of(k * rows, rows), rows)
            win_ref[dst, :] = stage_in[slot].astype(BF16)
            wout_ref[dst, :] = stage_out[slot].astype(BF16)
            return carry
        lax.fori_loop(0, n_slabs, slab, 0)

    def cast_ffn_weights():
        wup_bf_ref[...] = wup_ref[...].astype(BF16)
        wdn_bf_ref[...] = wdn_ref[...].astype(BF16)

    def project_pieces():
        xf = x1_ref[...]
        n_buf[...] = (xf * _rms_scale(xf) * g_ref[...]).astype(BF16)
        col = 0
        for dst in (u_nx, v_nx, q_nx, kv_nx):
            for c0 in range(0, dst.shape[1], MXU_COLS):
                def piece(dst=dst, c0=c0, col=col):
                    dst[:, c0:c0 + MXU_COLS] = jnp.dot(
                        n_buf[...], win_ref[:, col + c0:col + c0 + MXU_COLS],
                        preferred_element_type=F32).astype(BF16)
                yield piece
            col += dst.shape[1]

    def take_projection():
        kvbuf[0:CHUNK, :] = kvbuf[tq:tq + CHUNK, :]
        kvbuf[CHUNK:CHUNK + tq, :] = kv_nx[...]
        u_cu[...] = u_nx[...]
        v_cu[...] = v_nx[...]
        q_cu[...] = q_nx[...]

    def mix_a(c):
        rows = slice(c * CHUNK, (c + 1) * CHUNK)
        ss_a = jnp.zeros((CHUNK, A_GROUP_DIM), F32)
        for g in range(A_GROUPS):
            cols = slice(g * A_GROUP_DIM, (g + 1) * A_GROUP_DIM)
            ug = _gelu(u_cu[rows, cols].astype(F32))
            vg = _gelu(v_cu[rows, cols].astype(F32))
            mu = jnp.mean(vg, axis=-1, keepdims=True)
            vc = vg - mu
            var = jnp.mean(vc * vc, axis=-1, keepdims=True)
            vn = vc * lax.rsqrt(var + EPS) * lng_ref[:, cols] + lnb_ref[:, cols]
            mixed = jnp.dot(ws_bf[g], vn.astype(BF16), preferred_element_type=F32) + bs_ref[g]
            ag = ug * mixed
            a_buf[c, :, cols] = ag
            ss_a = ss_a + ag * ag
            yield
        scale_a = lax.rsqrt(jnp.sum(ss_a, axis=-1, keepdims=True) * (1.0 / A_WIDTH) + EPS)
        mix_nx[rows, 0:A_WIDTH] = (a_buf[c] * scale_a * ga_ref[...]).astype(BF16)
        yield

    def mix_b(c):
        rows = slice(c * CHUNK, (c + 1) * CHUNK)
        band = kvbuf[c * CHUNK:(c + 2) * CHUNK, :]
        k_stacks = _half_swapped_pair(band[:, :KV_WIDTH].astype(F32) * (HEAD_DIM ** -0.5 * LOG2E))
        v_stacks = _half_swapped_pair(band[:, KV_WIDTH:].astype(F32))
        chunk_id = (step - 1) * n_chunks + c
        first = (chunk_id % chunks_per_seq == 0).astype(jnp.int32)
        lane = lax.broadcasted_iota(jnp.int32, (CHUNK, 2 * HEAD_DIM), 1)
        lo = lane < HEAD_DIM
        srow = lax.broadcasted_iota(jnp.int32, (4 * CHUNK, 2 * HEAD_DIM), 0)
        slane = lax.broadcasted_iota(jnp.int32, (4 * CHUNK, 2 * HEAD_DIM), 1)
        row_sum_cols = ((srow < 2 * CHUNK) == (slane < HEAD_DIM)).astype(BF16)
        ss_b = jnp.zeros((CHUNK, 2 * HEAD_DIM), F32)
        for g in range(B_KV_HEADS):
            q_stack = jnp.concatenate(
                [q_cu[rows, (PAIRS_PER_KV * g + p) * 128:(PAIRS_PER_KV * g + p + 1) * 128]
                 for p in range(PAIRS_PER_KV)], axis=0)
            s_all = lax.dot_general(q_stack, k_stacks[g].astype(BF16),
                                    (((1,), (1,)), ((), ())),
                                    preferred_element_type=F32)
            yield
            p_rows = []
            sink_terms = []
            for p in range(PAIRS_PER_KV):
                p_cols = []
                sink_pair = []
                for par in range(2):
                    head = Q_PER_KV * g + 2 * p + par
                    rs = slice(p * CHUNK, (p + 1) * CHUNK)
                    cs = slice(par * 2 * CHUNK, (par + 1) * 2 * CHUNK)
                    s = s_all[rs, cs] + bias_ref[first, g, rs, cs]
                    sink = sinks_ref[head] * LOG2E
                    m = jnp.maximum(jnp.max(s, axis=-1, keepdims=True), sink)
                    p_cols.append(jnp.exp2(s - m).astype(BF16))
                    sink_pair.append(jnp.exp2(sink - m))
                p_rows.append(jnp.concatenate(p_cols, axis=1))
                sink_terms.append(sink_pair)
                yield
            probs = jnp.concatenate(p_rows, axis=0)
            values = jnp.concatenate([v_stacks[g].astype(BF16), row_sum_cols], axis=1)
            o_all = jnp.dot(probs, values, preferred_element_type=F32)
            for p in range(PAIRS_PER_KV):
                blk = PAIRS_PER_KV * g + p
                o_p = o_all[p * CHUNK:(p + 1) * CHUNK]
                den = o_p[:, 2 * HEAD_DIM:] + jnp.where(lo, sink_terms[p][0], sink_terms[p][1])
                o = o_p[:, :2 * HEAD_DIM] / den
                b_buf[c, :, blk * 128:(blk + 1) * 128] = o
                ss_b = ss_b + o * o
            yield
        scale_b = lax.rsqrt(jnp.sum(ss_b, axis=-1, keepdims=True) * (1.0 / B_WIDTH) + EPS)
        mix_nx[rows, A_WIDTH:MIX_WIDTH] = (b_buf[c] * scale_b * gb_ref[...]).astype(BF16)
        yield

    def mix_pieces():
        for c in range(n_chunks):
            gens = [mix_b(c), mix_a(c)]
            while gens:
                for gen in list(gens):
                    try:
                        next(gen)
                        yield
                    except StopIteration:
                        gens.remove(gen)

    def output_pieces():
        mix_cu[...] = mix_nx[...]
        for c0 in range(0, D_MODEL, MXU_COLS):
            def piece(c0=c0):
                o_ref[:, c0:c0 + MXU_COLS] = x3_ref[:, c0:c0 + MXU_COLS] + jnp.dot(
                    mix_cu[...], wout_ref[:, c0:c0 + MXU_COLS], preferred_element_type=F32)
            yield piece

    def run():
        out_gen = output_pieces()
        next(out_gen)()
        matmuls = list(out_gen)
        take_projection()
        matmuls += list(project_pieces())
        cast_ffn_weights()
        n_yields = n_chunks * MIX_PIECES_PER_CHUNK
        spread = len(matmuls) - TRAILING_MATMULS
        done = 0
        for k, _ in enumerate(mix_pieces(), start=1):
            want = (k * spread) // n_yields
            while done < want:
                matmuls[done]()
                done += 1
        for piece in matmuls[done:]:
            piece()

    @pl.when(step == 0)
    def _():
        row = lax.broadcasted_iota(jnp.int32, (CHUNK, CHUNK), 0)
        col = lax.broadcasted_iota(jnp.int32, (CHUNK, CHUNK), 1)
        for g in range(A_GROUPS):
            ws_bf[g] = jnp.where(row >= col, ws_ref[g], 0.0).astype(BF16)
        for ref in (kvbuf, u_nx, v_nx, q_nx, kv_nx, mix_nx):
            ref[...] = jnp.zeros_like(ref)
        load_projection_weights()

    run()


def _mix_block(x2, g, w_in, lng, lnb, w_s, b_s, sinks, bias, ga, gb, w_out, w_up, w_down,
               layer, tq, seq):
    t = x2.shape[0]
    n_tiles = t // tq
    n_chunks = tq // CHUNK
    proj = lambda w: pltpu.VMEM((tq, w), BF16)
    out_tile = lambda s: (jnp.clip(s - 2, 0, n_tiles - 1), 0)
    up_rows, dn_rows = D_MODEL // n_tiles, D_FF // n_tiles
    slab = lambda s: (jnp.minimum(s, n_tiles - 1), 0)
    return pl.pallas_call(
        functools.partial(_mix_block_kernel, layer, seq // CHUNK),
        out_shape=[jax.ShapeDtypeStruct((t, D_MODEL), F32),
                   jax.ShapeDtypeStruct((D_MODEL, D_FF), BF16),
                   jax.ShapeDtypeStruct((D_FF, D_MODEL), BF16)],
        grid=(n_tiles + 2,),
        in_specs=[
            pl.BlockSpec((tq, D_MODEL), lambda s: (jnp.minimum(s, n_tiles - 1), 0)),
            pl.BlockSpec((tq, D_MODEL), out_tile),
            _const_spec((1, D_MODEL)),
            pl.BlockSpec(memory_space=pl.ANY),
            _const_spec((1, A_WIDTH)),
            _const_spec((1, A_WIDTH)),
            _const_spec(w_s.shape),
            _const_spec(b_s.shape),
            pl.BlockSpec(memory_space=pltpu.SMEM),
            _const_spec(bias.shape),
            _const_spec((1, A_WIDTH)),
            _const_spec((1, B_WIDTH)),
            pl.BlockSpec(memory_space=pl.ANY),
            pl.BlockSpec((None, up_rows, D_FF), lambda s: (layer,) + slab(s)),
            pl.BlockSpec((None, dn_rows, D_MODEL), lambda s: (layer,) + slab(s)),
        ],
        out_specs=[pl.BlockSpec((tq, D_MODEL), out_tile),
                   pl.BlockSpec((up_rows, D_FF), slab),
                   pl.BlockSpec((dn_rows, D_MODEL), slab)],
        scratch_shapes=[
            pltpu.VMEM((A_GROUPS, CHUNK, CHUNK), BF16),
            pltpu.VMEM((D_MODEL, w_in.shape[2]), BF16),
            pltpu.VMEM((MIX_WIDTH, D_MODEL), BF16),
            pltpu.VMEM((2, WEIGHT_STAGE_ROWS, w_in.shape[2]), F32),
            pltpu.VMEM((2, WEIGHT_STAGE_ROWS, D_MODEL), F32),
            pltpu.SemaphoreType.DMA((2, 2)),
            proj(D_MODEL),
            proj(A_WIDTH), proj(A_WIDTH), proj(B_WIDTH), proj(2 * KV_WIDTH),
            proj(A_WIDTH), proj(A_WIDTH), proj(B_WIDTH),
            pltpu.VMEM((tq + CHUNK, 2 * KV_WIDTH), BF16),
            proj(MIX_WIDTH), proj(MIX_WIDTH),
            pltpu.VMEM((n_chunks, CHUNK, A_WIDTH), F32),
            pltpu.VMEM((n_chunks, CHUNK, B_WIDTH), F32),
        ],
        compiler_params=_params(1),
        name="mix_block",
    )(x2, x2, g, w_in, lng, lnb, w_s, b_s, sinks, bias, ga, gb, w_out, w_up, w_down)


def _ffn_kernel(final_norm, h_hbm, g_ref, wu_ref, wd_ref, gf_ref, o_hbm,
                acc, nbuf, sem_in, sem_out):
    i = pl.program_id(0)
    f = pl.program_id(1)
    n_tiles = pl.num_programs(0)
    last_f = pl.num_programs(1) - 1
    tm = acc.shape[1]
    n_row_chunks = tm // FFN_NORM_ROWS
    slot = i % 2
    other = 1 - slot

    def h_copy(tile, s):
        return pltpu.make_async_copy(h_hbm.at[pl.ds(tile * tm, tm)], acc.at[s], sem_in)

    def o_copy(tile, s):
        return pltpu.make_async_copy(acc.at[s], o_hbm.at[pl.ds(tile * tm, tm)], sem_out.at[s])

    def chunk(k):
        return pl.ds(pl.multiple_of(k * FFN_NORM_ROWS, FFN_NORM_ROWS), FFN_NORM_ROWS)

    @pl.when(f == 0)
    def _():
        @pl.when(i == 0)
        def _():
            h_copy(0, 0).start()
        h_copy(i, slot).wait()

        def norm_rows(k, carry):
            hf = acc[slot, chunk(k), :]
            nbuf[chunk(k), :] = (hf * _rms_scale(hf) * g_ref[...]).astype(BF16)
            return carry
        lax.fori_loop(0, n_row_chunks, norm_rows, 0, unroll=True)

    @pl.when((f == 1) & (i + 1 < n_tiles))
    def _():
        @pl.when(i >= 1)
        def _():
            o_copy(i - 1, other).wait()
        h_copy(i + 1, other).start()

    z = jnp.maximum(jnp.dot(nbuf[...], wu_ref[...], preferred_element_type=F32), 0.0)
    update = jnp.dot((z * z).astype(BF16), wd_ref[...], preferred_element_type=F32)
    acc[slot] = acc[slot] + update

    @pl.when(f == last_f)
    def _():
        if final_norm:
            def final_rows(k, carry):
                y = acc[slot, chunk(k), :]
                acc[slot, chunk(k), :] = y * _rms_scale(y) * gf_ref[...]
                return carry
            lax.fori_loop(0, n_row_chunks, final_rows, 0, unroll=True)
        o_copy(i, slot).start()

        @pl.when(i == n_tiles - 1)
        def _():
            @pl.when(i >= 1)
            def _():
                o_copy(i - 1, other).wait()
            o_copy(i, slot).wait()


def _ffn(h, g, w_up, w_down, gf, final_norm, tm, tf):
    t = h.shape[0]
    assert D_FF // tf >= 2 and tm % FFN_NORM_ROWS == 0
    return pl.pallas_call(
        functools.partial(_ffn_kernel, final_norm),
        out_shape=jax.ShapeDtypeStruct((t, D_MODEL), F32),
        grid=(t // tm, D_FF // tf),
        in_specs=[
            pl.BlockSpec(memory_space=pl.ANY),
            _const_spec((1, D_MODEL)),
            pl.BlockSpec((D_MODEL, tf), lambda i, f: (0, f)),
            pl.BlockSpec((tf, D_MODEL), lambda i, f: (f, 0)),
            _const_spec((1, D_MODEL)),
        ],
        out_specs=pl.BlockSpec(memory_space=pl.ANY),
        scratch_shapes=[
            pltpu.VMEM((2, tm, D_MODEL), F32),
            pltpu.VMEM((tm, D_MODEL), BF16),
            pltpu.SemaphoreType.DMA(()),
            pltpu.SemaphoreType.DMA((2,)),
        ],
        compiler_params=_params(2),
        name="ffn",
    )(h, g, w_up, w_down, gf)


def _relative_buckets():
    i = jnp.arange(CHUNK)[:, None]
    j = jnp.arange(2 * CHUNK)[None, :]
    rel = i + CHUNK - j
    relc = jnp.maximum(rel, 0)
    n_exact = N_BUCKETS // 2
    relf = jnp.maximum(relc, n_exact).astype(F32)
    large = n_exact + (jnp.log(relf / n_exact) / math.log(MAX_DISTANCE / n_exact)
                       * (N_BUCKETS - n_exact)).astype(jnp.int32)
    large = jnp.minimum(large, N_BUCKETS - 1)
    bucket = jnp.where(relc < n_exact, relc, large)
    return jnp.where((rel >= 0) & (rel < CHUNK), bucket, -1).astype(jnp.int32)


def kernel(x, rel_bias_table, mix_norm_g, w_in, gate_norm_g, gate_norm_b, w_spatial, b_spatial,
           attn_sinks, out_norm_a_g, out_norm_b_g, w_out, ffn_norm_g, w_up, w_down, final_norm_g):
    bsz, seq, d = x.shape
    depth = w_in.shape[0]
    t = bsz * seq
    bias = _attention_bias(rel_bias_table.astype(F32), _relative_buckets())
    h = x.reshape(t, d)
    for layer in range(depth):
        b_s = jnp.broadcast_to(b_spatial[layer][:, :, None], (A_GROUPS, CHUNK, A_GROUP_DIM))
        h_mid, w_up_bf, w_down_bf = _mix_block(
            h, mix_norm_g[layer].reshape(1, d), w_in,
            gate_norm_g[layer].reshape(1, A_WIDTH), gate_norm_b[layer].reshape(1, A_WIDTH),
            w_spatial[layer], b_s, attn_sinks[layer], bias,
            out_norm_a_g[layer].reshape(1, A_WIDTH), out_norm_b_g[layer].reshape(1, B_WIDTH),
            w_out, w_up, w_down, layer=layer, tq=256, seq=seq)
        h = _ffn(h_mid, ffn_norm_g[layer].reshape(1, d), w_up_bf, w_down_bf,
                 final_norm_g.reshape(1, d), final_norm=layer == depth - 1, tm=1024, tf=2048)
    return h.reshape(bsz, seq, d)
```

```python
import functools
import math

import jax
import jax.numpy as jnp
from jax import lax
from jax.experimental import pallas as pl
from jax.experimental.pallas import tpu as pltpu

D_MODEL = 2048
CHUNK = 128
A_GROUPS = 8
A_GROUP_DIM = 128
A_WIDTH = A_GROUPS * A_GROUP_DIM
HEAD_DIM = 64
B_HEADS = 16
B_KV_HEADS = 2
Q_PER_KV = B_HEADS // B_KV_HEADS
PAIRS_PER_KV = Q_PER_KV // 2
B_WIDTH = B_HEADS * HEAD_DIM
KV_WIDTH = B_KV_HEADS * HEAD_DIM
N_BUCKETS = 32
MAX_DISTANCE = 128
MIX_WIDTH = A_WIDTH + B_WIDTH
D_FF = 4 * D_MODEL
EPS = 1e-5
NEG = -1e30
LOG2E = 1.4426950408889634
GELU_TANH_SCALE = math.sqrt(2.0 / math.pi)
GELU_CUBIC = 0.044715

V7X_VMEM_LIMIT_BYTES = 62 * 1024 * 1024
MXU_COLS = 256
FFN_NORM_ROWS = 128
TRAILING_MATMULS = 2
WEIGHT_STAGE_ROWS = 128
MIX_B_PIECES = B_KV_HEADS * (PAIRS_PER_KV + 2) + 1

BF16 = jnp.bfloat16
F32 = jnp.float32


def _rms_scale(xf):
    return lax.rsqrt(jnp.mean(xf * xf, axis=-1, keepdims=True) + EPS)


def _gelu(x):
    a = -2.0 * GELU_TANH_SCALE * LOG2E
    return x / (1.0 + jnp.exp2(x * (x * x * (a * GELU_CUBIC) + a)))


def _params(n_axes):
    return pltpu.CompilerParams(
        dimension_semantics=("arbitrary",) * n_axes,
        vmem_limit_bytes=V7X_VMEM_LIMIT_BYTES)


def _const_spec(shape):
    return pl.BlockSpec(shape, lambda *_: (0,) * len(shape), pipeline_mode=pl.Buffered(1))


def _bias_kernel(table_ref, bucket_ref, o_ref):
    bucket = bucket_ref[...]
    col = lax.broadcasted_iota(jnp.int32, bucket.shape, 1)

    def one_kv_head(g, carry):
        for p in range(PAIRS_PER_KV):
            for par in range(2):
                h = g * Q_PER_KV + 2 * p + par
                val = jnp.full(bucket.shape, NEG, F32)
                for b in range(N_BUCKETS):
                    val = jnp.where(bucket == b, table_ref[b, h] * LOG2E, val)
                rs = slice(p * CHUNK, (p + 1) * CHUNK)
                cs = slice(par * 2 * CHUNK, (par + 1) * 2 * CHUNK)
                o_ref[0, g, rs, cs] = val
                o_ref[1, g, rs, cs] = jnp.where(col < CHUNK, NEG, val)
        return carry
    lax.fori_loop(0, B_KV_HEADS, one_kv_head, 0)


def _attention_bias(rel_bias_table, bucket):
    shape = (2, B_KV_HEADS, PAIRS_PER_KV * CHUNK, 4 * CHUNK)
    return pl.pallas_call(
        _bias_kernel,
        out_shape=jax.ShapeDtypeStruct(shape, F32),
        grid=(1,),
        in_specs=[
            pl.BlockSpec(memory_space=pltpu.SMEM),
            pl.BlockSpec((CHUNK, 2 * CHUNK), lambda s: (0, 0)),
        ],
        out_specs=pl.BlockSpec(shape, lambda s: (0, 0, 0, 0)),
        compiler_params=_params(1),
        name="attn_bias",
    )(rel_bias_table, bucket)


def _half_swapped_pair(t):
    lane = lax.broadcasted_iota(jnp.int32, t.shape, 1)
    lo = lane < HEAD_DIM
    tr = pltpu.roll(t, HEAD_DIM, axis=1)
    zero = jnp.zeros_like(t)
    kv0 = jnp.concatenate([jnp.where(lo, t, zero), jnp.where(lo, zero, tr)], axis=0)
    kv1 = jnp.concatenate([jnp.where(lo, tr, zero), jnp.where(lo, zero, t)], axis=0)
    return kv0, kv1


def _mix_block_kernel(layer, chunks_per_seq,
                      x1_ref, x3_ref, g_ref, win_hbm, lng_ref, lnb_ref, ws_ref, bs_ref,
                      sinks_ref, bias_ref, ga_ref, gb_ref, wout_hbm, wup_ref, wdn_ref,
                      o_ref, wup_bf_ref, wdn_bf_ref,
                      ws_bf, win_ref, wout_ref, stage_in, stage_out, stage_sem,
                      n_buf, u_nx, v_nx, q_nx, kv_nx, u_cu, v_cu, q_cu, kvbuf,
                      mix_nx, mix_cu, a_buf, b_buf):
    tq = x1_ref.shape[0]
    n_chunks = tq // CHUNK
    step = pl.program_id(0)

    def load_projection_weights():
        rows = stage_in.shape[1]
        n_slabs = D_MODEL // rows

        def copies(k, slot):
            src = pl.ds(k * rows, rows)
            return (pltpu.make_async_copy(win_hbm.at[layer, src], stage_in.at[slot],
                                          stage_sem.at[0, slot]),
                    pltpu.make_async_copy(wout_hbm.at[layer, src], stage_out.at[slot],
                                          stage_sem.at[1, slot]))

        for cp in copies(0, 0):
            cp.start()

        def slab(k, carry):
            slot = k % 2

            @pl.when(k + 1 < n_slabs)
            def _():
                for cp in copies(k + 1, 1 - slot):
                    cp.start()
            for cp in copies(k, slot):
                cp.wait()
            dst = pl.ds(pl.multiple_of(k * rows, rows), rows)
            win_ref[dst, :] = stage_in[slot].astype(BF16)
            wout_ref[dst, :] = stage_out[slot].astype(BF16)
            return carry
        lax.fori_loop(0, n_slabs, slab, 0)

    def cast_ffn_weights():
        wup_bf_ref[...] = wup_ref[...].astype(BF16)
        wdn_bf_ref[...] = wdn_ref[...].astype(BF16)

    def project_pieces():
        xf = x1_ref[...]
        n_buf[...] = (xf * _rms_scale(xf) * g_ref[...]).astype(BF16)
        col = 0
        for dst in (u_nx, v_nx, q_nx, kv_nx):
            for c0 in range(0, dst.shape[1], MXU_COLS):
                def piece(dst=dst, c0=c0, col=col):
                    dst[:, c0:c0 + MXU_COLS] = jnp.dot(
                        n_buf[...], win_ref[:, col + c0:col + c0 + MXU_COLS],
                        preferred_element_type=F32).astype(BF16)
                yield piece
            col += dst.shape[1]

    def take_projection():
        kvbuf[0:CHUNK, :] = kvbuf[tq:tq + CHUNK, :]
        kvbuf[CHUNK:CHUNK + tq, :] = kv_nx[...]
        u_cu[...] = u_nx[...]
        v_cu[...] = v_nx[...]
        q_cu[...] = q_nx[...]

    def mix_a():
        ss_a = [jnp.zeros((CHUNK, A_GROUP_DIM), F32) for _ in range(n_chunks)]
        for g in range(A_GROUPS):
            cols = slice(g * A_GROUP_DIM, (g + 1) * A_GROUP_DIM)
            ugs, vns = [], []
            for c in range(n_chunks):
                rows = slice(c * CHUNK, (c + 1) * CHUNK)
                ugs.append(_gelu(u_cu[rows, cols].astype(F32)))
                vg = _gelu(v_cu[rows, cols].astype(F32))
                mu = jnp.mean(vg, axis=-1, keepdims=True)
                vc = vg - mu
                var = jnp.mean(vc * vc, axis=-1, keepdims=True)
                vn = vc * lax.rsqrt(var + EPS) * lng_ref[:, cols] + lnb_ref[:, cols]
                vns.append(vn.astype(BF16))
                yield
            mixed = jnp.dot(ws_bf[g], jnp.concatenate(vns, axis=1), preferred_element_type=F32)
            for c in range(n_chunks):
                ag = ugs[c] * (mixed[:, c * A_GROUP_DIM:(c + 1) * A_GROUP_DIM] + bs_ref[g])
                a_buf[c, :, cols] = ag
                ss_a[c] = ss_a[c] + ag * ag
            yield
        for c in range(n_chunks):
            rows = slice(c * CHUNK, (c + 1) * CHUNK)
            scale_a = lax.rsqrt(jnp.sum(ss_a[c], axis=-1, keepdims=True) * (1.0 / A_WIDTH) + EPS)
            mix_nx[rows, 0:A_WIDTH] = (a_buf[c] * scale_a * ga_ref[...]).astype(BF16)
            yield

    def mix_b(c):
        rows = slice(c * CHUNK, (c + 1) * CHUNK)
        band = kvbuf[c * CHUNK:(c + 2) * CHUNK, :]
        k_stacks = _half_swapped_pair(band[:, :KV_WIDTH].astype(F32) * (HEAD_DIM ** -0.5 * LOG2E))
        v_stacks = _half_swapped_pair(band[:, KV_WIDTH:].astype(F32))
        chunk_id = (step - 1) * n_chunks + c
        first = (chunk_id % chunks_per_seq == 0).astype(jnp.int32)
        lane = lax.broadcasted_iota(jnp.int32, (CHUNK, 2 * HEAD_DIM), 1)
        lo = lane < HEAD_DIM
        srow = lax.broadcasted_iota(jnp.int32, (4 * CHUNK, 2 * HEAD_DIM), 0)
        slane = lax.broadcasted_iota(jnp.int32, (4 * CHUNK, 2 * HEAD_DIM), 1)
        row_sum_cols = ((srow < 2 * CHUNK) == (slane < HEAD_DIM)).astype(BF16)
        ss_b = jnp.zeros((CHUNK, 2 * HEAD_DIM), F32)
        for g in range(B_KV_HEADS):
            q_stack = jnp.concatenate(
                [q_cu[rows, (PAIRS_PER_KV * g + p) * 128:(PAIRS_PER_KV * g + p + 1) * 128]
                 for p in range(PAIRS_PER_KV)], axis=0)
            s_all = lax.dot_general(q_stack, k_stacks[g].astype(BF16),
                                    (((1,), (1,)), ((), ())),
                                    preferred_element_type=F32)
            yield
            p_rows = []
            sink_terms = []
            for p in range(PAIRS_PER_KV):
                p_cols = []
                sink_pair = []
                for par in range(2):
                    head = Q_PER_KV * g + 2 * p + par
                    rs = slice(p * CHUNK, (p + 1) * CHUNK)
                    cs = slice(par * 2 * CHUNK, (par + 1) * 2 * CHUNK)
                    s = s_all[rs, cs] + bias_ref[first, g, rs, cs]
                    sink = sinks_ref[head] * LOG2E
                    m = jnp.maximum(jnp.max(s, axis=-1, keepdims=True), sink)
                    p_cols.append(jnp.exp2(s - m).astype(BF16))
                    sink_pair.append(jnp.exp2(sink - m))
                p_rows.append(jnp.concatenate(p_cols, axis=1))
                sink_terms.append(sink_pair)
                yield
            probs = jnp.concatenate(p_rows, axis=0)
            values = jnp.concatenate([v_stacks[g].astype(BF16), row_sum_cols], axis=1)
            o_all = jnp.dot(probs, values, preferred_element_type=F32)
            for p in range(PAIRS_PER_KV):
                blk = PAIRS_PER_KV * g + p
                o_p = o_all[p * CHUNK:(p + 1) * CHUNK]
                den = o_p[:, 2 * HEAD_DIM:] + jnp.where(lo, sink_terms[p][0], sink_terms[p][1])
                o = o_p[:, :2 * HEAD_DIM] / den
                b_buf[c, :, blk * 128:(blk + 1) * 128] = o
                ss_b = ss_b + o * o
            yield
        scale_b = lax.rsqrt(jnp.sum(ss_b, axis=-1, keepdims=True) * (1.0 / B_WIDTH) + EPS)
        mix_nx[rows, A_WIDTH:MIX_WIDTH] = (b_buf[c] * scale_b * gb_ref[...]).astype(BF16)
        yield

    def mix_pieces():
        attention = (mix_b(c) for c in range(n_chunks))
        gens = [next(attention), mix_a()]
        while gens:
            for gen in list(gens):
                try:
                    next(gen)
                    yield
                except StopIteration:
                    following = next(attention, None) if gen is gens[0] else None
                    if following is None:
                        gens.remove(gen)
                    else:
                        gens[0] = following

    def output_pieces():
        mix_cu[...] = mix_nx[...]
        for c0 in range(0, D_MODEL, MXU_COLS):
            def piece(c0=c0):
                o_ref[:, c0:c0 + MXU_COLS] = x3_ref[:, c0:c0 + MXU_COLS] + jnp.dot(
                    mix_cu[...], wout_ref[:, c0:c0 + MXU_COLS], preferred_element_type=F32)
            yield piece

    def run():
        out_gen = output_pieces()
        next(out_gen)()
        matmuls = list(out_gen)
        take_projection()
        matmuls += list(project_pieces())
        cast_ffn_weights()
        n_yields = n_chunks * MIX_B_PIECES + A_GROUPS * (n_chunks + 1) + n_chunks
        spread = len(matmuls) - TRAILING_MATMULS
        done = 0
        for k, _ in enumerate(mix_pieces(), start=1):
            want = (k * spread) // n_yields
            while done < want:
                matmuls[done]()
                done += 1
        for piece in matmuls[done:]:
            piece()

    @pl.when(step == 0)
    def _():
        row = lax.broadcasted_iota(jnp.int32, (CHUNK, CHUNK), 0)
        col = lax.broadcasted_iota(jnp.int32, (CHUNK, CHUNK), 1)
        for g in range(A_GROUPS):
            ws_bf[g] = jnp.where(row >= col, ws_ref[g], 0.0).astype(BF16)
        for ref in (kvbuf, u_nx, v_nx, q_nx, kv_nx, mix_nx):
            ref[...] = jnp.zeros_like(ref)
        load_projection_weights()

    run()


def _mix_block(x2, g, w_in, lng, lnb, w_s, b_s, sinks, bias, ga, gb, w_out, w_up, w_down,
               layer, tq, seq):
    t = x2.shape[0]
    n_tiles = t // tq
    n_chunks = tq // CHUNK
    proj = lambda w: pltpu.VMEM((tq, w), BF16)
    out_tile = lambda s: (jnp.clip(s - 2, 0, n_tiles - 1), 0)
    up_rows, dn_rows = D_MODEL // n_tiles, D_FF // n_tiles
    slab = lambda s: (jnp.minimum(s, n_tiles - 1), 0)
    return pl.pallas_call(
        functools.partial(_mix_block_kernel, layer, seq // CHUNK),
        out_shape=[jax.ShapeDtypeStruct((t, D_MODEL), F32),
                   jax.ShapeDtypeStruct((D_MODEL, D_FF), BF16),
                   jax.ShapeDtypeStruct((D_FF, D_MODEL), BF16)],
        grid=(n_tiles + 2,),
        in_specs=[
            pl.BlockSpec((tq, D_MODEL), lambda s: (jnp.minimum(s, n_tiles - 1), 0)),
            pl.BlockSpec((tq, D_MODEL), out_tile),
            _const_spec((1, D_MODEL)),
            pl.BlockSpec(memory_space=pl.ANY),
            _const_spec((1, A_WIDTH)),
            _const_spec((1, A_WIDTH)),
            _const_spec(w_s.shape),
            _const_spec(b_s.shape),
            pl.BlockSpec(memory_space=pltpu.SMEM),
            _const_spec(bias.shape),
            _const_spec((1, A_WIDTH)),
            _const_spec((1, B_WIDTH)),
            pl.BlockSpec(memory_space=pl.ANY),
            pl.BlockSpec((None, up_rows, D_FF), lambda s: (layer,) + slab(s)),
            pl.BlockSpec((None, dn_rows, D_MODEL), lambda s: (layer,) + slab(s)),
        ],
        out_specs=[pl.BlockSpec((tq, D_MODEL), out_tile),
                   pl.BlockSpec((up_rows, D_FF), slab),
                   pl.BlockSpec((dn_rows, D_MODEL), slab)],
        scratch_shapes=[
            pltpu.VMEM((A_GROUPS, CHUNK, CHUNK), BF16),
            pltpu.VMEM((D_MODEL, w_in.shape[2]), BF16),
            pltpu.VMEM((MIX_WIDTH, D_MODEL), BF16),
            pltpu.VMEM((2, WEIGHT_STAGE_ROWS, w_in.shape[2]), F32),
            pltpu.VMEM((2, WEIGHT_STAGE_ROWS, D_MODEL), F32),
            pltpu.SemaphoreType.DMA((2, 2)),
            proj(D_MODEL),
            proj(A_WIDTH), proj(A_WIDTH), proj(B_WIDTH), proj(2 * KV_WIDTH),
            proj(A_WIDTH), proj(A_WIDTH), proj(B_WIDTH),
            pltpu.VMEM((tq + CHUNK, 2 * KV_WIDTH), BF16),
            proj(MIX_WIDTH), proj(MIX_WIDTH),
            pltpu.VMEM((n_chunks, CHUNK, A_WIDTH), F32),
            pltpu.VMEM((n_chunks, CHUNK, B_WIDTH), F32),
        ],
        compiler_params=_params(1),
        name="mix_block",
    )(x2, x2, g, w_in, lng, lnb, w_s, b_s, sinks, bias, ga, gb, w_out, w_up, w_down)


def _ffn_kernel(final_norm, h_hbm, g_ref, wu_ref, wd_ref, gf_ref, o_hbm,
                acc, nbuf, sem_in, sem_out):
    i = pl.program_id(0)
    f = pl.program_id(1)
    n_tiles = pl.num_programs(0)
    last_f = pl.num_programs(1) - 1
    tm = acc.shape[1]
    n_row_chunks = tm // FFN_NORM_ROWS
    slot = i % 2
    other = 1 - slot

    def h_copy(tile, s):
        return pltpu.make_async_copy(h_hbm.at[pl.ds(tile * tm, tm)], acc.at[s], sem_in)

    def o_copy(tile, s):
        return pltpu.make_async_copy(acc.at[s], o_hbm.at[pl.ds(tile * tm, tm)], sem_out.at[s])

    def chunk(k):
        return pl.ds(pl.multiple_of(k * FFN_NORM_ROWS, FFN_NORM_ROWS), FFN_NORM_ROWS)

    @pl.when(f == 0)
    def _():
        @pl.when(i == 0)
        def _():
            h_copy(0, 0).start()
        h_copy(i, slot).wait()

        def norm_rows(k, carry):
            hf = acc[slot, chunk(k), :]
            nbuf[chunk(k), :] = (hf * _rms_scale(hf) * g_ref[...]).astype(BF16)
            return carry
        lax.fori_loop(0, n_row_chunks, norm_rows, 0, unroll=True)

    @pl.when((f == 1) & (i + 1 < n_tiles))
    def _():
        @pl.when(i >= 1)
        def _():
            o_copy(i - 1, other).wait()
        h_copy(i + 1, other).start()

    z = jnp.maximum(jnp.dot(nbuf[...], wu_ref[...], preferred_element_type=F32), 0.0)
    update = jnp.dot((z * z).astype(BF16), wd_ref[...], preferred_element_type=F32)
    acc[slot] = acc[slot] + update

    @pl.when(f == last_f)
    def _():
        if final_norm:
            def final_rows(k, carry):
                y = acc[slot, chunk(k), :]
                acc[slot, chunk(k), :] = y * _rms_scale(y) * gf_ref[...]
                return carry
            lax.fori_loop(0, n_row_chunks, final_rows, 0, unroll=True)
        o_copy(i, slot).start()

        @pl.when(i == n_tiles - 1)
        def _():
            @pl.when(i >= 1)
            def _():
                o_copy(i - 1, other).wait()
            o_copy(i, slot).wait()


def _ffn(h, g, w_up, w_down, gf, final_norm, tm, tf):
    t = h.shape[0]
    assert D_FF // tf >= 2 and tm % FFN_NORM_ROWS == 0
    return pl.pallas_call(
        functools.partial(_ffn_kernel, final_norm),
        out_shape=jax.ShapeDtypeStruct((t, D_MODEL), F32),
        grid=(t // tm, D_FF // tf),
        in_specs=[
            pl.BlockSpec(memory_space=pl.ANY),
            _const_spec((1, D_MODEL)),
            pl.BlockSpec((D_MODEL, tf), lambda i, f: (0, f)),
            pl.BlockSpec((tf, D_MODEL), lambda i, f: (f, 0)),
            _const_spec((1, D_MODEL)),
        ],
        out_specs=pl.BlockSpec(memory_space=pl.ANY),
        scratch_shapes=[
            pltpu.VMEM((2, tm, D_MODEL), F32),
            pltpu.VMEM((tm, D_MODEL), BF16),
            pltpu.SemaphoreType.DMA(()),
            pltpu.SemaphoreType.DMA((2,)),
        ],
        compiler_params=_params(2),
        name="ffn",
    )(h, g, w_up, w_down, gf)


def _relative_buckets():
    i = jnp.arange(CHUNK)[:, None]
    j = jnp.arange(2 * CHUNK)[None, :]
    rel = i + CHUNK - j
    relc = jnp.maximum(rel, 0)
    n_exact = N_BUCKETS // 2
    relf = jnp.maximum(relc, n_exact).astype(F32)
    large = n_exact + (jnp.log(relf / n_exact) / math.log(MAX_DISTANCE / n_exact)
                       * (N_BUCKETS - n_exact)).astype(jnp.int32)
    large = jnp.minimum(large, N_BUCKETS - 1)
    bucket = jnp.where(relc < n_exact, relc, large)
    return jnp.where((rel >= 0) & (rel < CHUNK), bucket, -1).astype(jnp.int32)


def kernel(x, rel_bias_table, mix_norm_g, w_in, gate_norm_g, gate_norm_b, w_spatial, b_spatial,
           attn_sinks, out_norm_a_g, out_norm_b_g, w_out, ffn_norm_g, w_up, w_down, final_norm_g):
    bsz, seq, d = x.shape
    depth = w_in.shape[0]
    t = bsz * seq
    bias = _attention_bias(rel_bias_table.astype(F32), _relative_buckets())
    h = x.reshape(t, d)
    for layer in range(depth):
        b_s = jnp.broadcast_to(b_spatial[layer][:, :, None], (A_GROUPS, CHUNK, A_GROUP_DIM))
        h_mid, w_up_bf, w_down_bf = _mix_block(
            h, mix_norm_g[layer].reshape(1, d), w_in,
            gate_norm_g[layer].reshape(1, A_WIDTH), gate_norm_b[layer].reshape(1, A_WIDTH),
            w_spatial[layer], b_s, attn_sinks[layer], bias,
            out_norm_a_g[layer].reshape(1, A_WIDTH), out_norm_b_g[layer].reshape(1, B_WIDTH),
            w_out, w_up, w_down, layer=layer, tq=256, seq=seq)
        h = _ffn(h_mid, ffn_norm_g[layer].reshape(1, d), w_up_bf, w_down_bf,
                 final_norm_g.reshape(1, d), final_norm=layer == depth - 1, tm=1024, tf=2048)
    return h.reshape(bsz, seq, d)
```

```python
import functools
import math

import jax
import jax.numpy as jnp
from jax import lax
from jax.experimental import pallas as pl
from jax.experimental.pallas import tpu as pltpu

D_MODEL = 2048
CHUNK = 128
A_GROUPS = 8
A_GROUP_DIM = 128
A_WIDTH = A_GROUPS * A_GROUP_DIM
HEAD_DIM = 64
B_HEADS = 16
B_KV_HEADS = 2
Q_PER_KV = B_HEADS // B_KV_HEADS
PAIRS_PER_KV = Q_PER_KV // 2
B_WIDTH = B_HEADS * HEAD_DIM
KV_WIDTH = B_KV_HEADS * HEAD_DIM
N_BUCKETS = 32
MAX_DISTANCE = 128
MIX_WIDTH = A_WIDTH + B_WIDTH
D_FF = 4 * D_MODEL
EPS = 1e-5
NEG = -1e30
LOG2E = 1.4426950408889634
GELU_TANH_SCALE = math.sqrt(2.0 / math.pi)
GELU_CUBIC = 0.044715

V7X_VMEM_LIMIT_BYTES = 62 * 1024 * 1024
MXU_COLS = 256
FFN_NORM_ROWS = 128
TRAILING_MATMULS = 2
WEIGHT_STAGE_ROWS = 128
MIX_B_PIECES = B_KV_HEADS * (PAIRS_PER_KV + 2) + 1

BF16 = jnp.bfloat16
F32 = jnp.float32


def _rms_scale(xf):
    return lax.rsqrt(jnp.mean(xf * xf, axis=-1, keepdims=True) + EPS)


def _gelu(x):
    a = -2.0 * GELU_TANH_SCALE * LOG2E
    return x / (1.0 + jnp.exp2(x * (x * x * (a * GELU_CUBIC) + a)))


def _params(n_axes):
    return pltpu.CompilerParams(
        dimension_semantics=("arbitrary",) * n_axes,
        vmem_limit_bytes=V7X_VMEM_LIMIT_BYTES)


def _const_spec(shape):
    return pl.BlockSpec(shape, lambda *_: (0,) * len(shape), pipeline_mode=pl.Buffered(1))


def _half_swapped_pair(t):
    lane = lax.broadcasted_iota(jnp.int32, t.shape, 1)
    lo = lane < HEAD_DIM
    tr = pltpu.roll(t, HEAD_DIM, axis=1)
    zero = jnp.zeros_like(t)
    kv0 = jnp.concatenate([jnp.where(lo, t, zero), jnp.where(lo, zero, tr)], axis=0)
    kv1 = jnp.concatenate([jnp.where(lo, tr, zero), jnp.where(lo, zero, t)], axis=0)
    return kv0, kv1


def _mix_block_kernel(layer, chunks_per_seq,
                      x1_ref, x3_ref, g_ref, win_hbm, lng_ref, lnb_ref, ws_ref, bsp_ref,
                      sinks_ref, table_ref, bucket_ref, ga_ref, gb_ref, wout_hbm, wup_ref, wdn_ref,
                      o_ref, wup_bf_ref, wdn_bf_ref,
                      ws_bf, bs_ref, bias_ref, win_ref, wout_ref, stage_in, stage_out, stage_sem,
                      n_buf, u_nx, v_nx, q_nx, kv_nx, u_cu, v_cu, q_cu, kvbuf,
                      mix_nx, mix_cu, a_buf, b_buf):
    tq = x1_ref.shape[0]
    n_chunks = tq // CHUNK
    step = pl.program_id(0)

    def head_bias(h):
        bucket = bucket_ref[...]
        col = lax.broadcasted_iota(jnp.int32, bucket.shape, 1)
        val = jnp.full(bucket.shape, NEG, F32)
        for b in range(N_BUCKETS):
            val = jnp.where(bucket == b, table_ref[b, h] * LOG2E, val)
        g, p, par = h // Q_PER_KV, (h % Q_PER_KV) // 2, h % 2
        bias_ref[0, g, p, par] = val
        bias_ref[1, g, p, par] = jnp.where(col < CHUNK, NEG, val)

    def load_projection_weights():
        rows = stage_in.shape[1]
        n_slabs = D_MODEL // rows
        assert n_slabs == B_HEADS

        def copies(k, slot):
            src = pl.ds(k * rows, rows)
            return (pltpu.make_async_copy(win_hbm.at[layer, src], stage_in.at[slot],
                                          stage_sem.at[0, slot]),
                    pltpu.make_async_copy(wout_hbm.at[layer, src], stage_out.at[slot],
                                          stage_sem.at[1, slot]))

        for cp in copies(0, 0):
            cp.start()

        def slab(k, carry):
            slot = k % 2

            @pl.when(k + 1 < n_slabs)
            def _():
                for cp in copies(k + 1, 1 - slot):
                    cp.start()
            head_bias(k)
            for cp in copies(k, slot):
                cp.wait()
            dst = pl.ds(pl.multiple_of(k * rows, rows), rows)
            win_ref[dst, :] = stage_in[slot].astype(BF16)
            wout_ref[dst, :] = stage_out[slot].astype(BF16)
            return carry
        lax.fori_loop(0, n_slabs, slab, 0)

    def cast_ffn_weights():
        wup_bf_ref[...] = wup_ref[...].astype(BF16)
        wdn_bf_ref[...] = wdn_ref[...].astype(BF16)

    def project_pieces():
        xf = x1_ref[...]
        n_buf[...] = (xf * _rms_scale(xf) * g_ref[...]).astype(BF16)
        col = 0
        for dst in (u_nx, v_nx, q_nx, kv_nx):
            for c0 in range(0, dst.shape[1], MXU_COLS):
                def piece(dst=dst, c0=c0, col=col):
                    dst[:, c0:c0 + MXU_COLS] = jnp.dot(
                        n_buf[...], win_ref[:, col + c0:col + c0 + MXU_COLS],
                        preferred_element_type=F32).astype(BF16)
                yield piece
            col += dst.shape[1]

    def take_projection():
        kvbuf[0:CHUNK, :] = kvbuf[tq:tq + CHUNK, :]
        kvbuf[CHUNK:CHUNK + tq, :] = kv_nx[...]
        u_cu[...] = u_nx[...]
        v_cu[...] = v_nx[...]
        q_cu[...] = q_nx[...]

    def mix_a():
        ss_a = [jnp.zeros((CHUNK, A_GROUP_DIM), F32) for _ in range(n_chunks)]
        for g in range(A_GROUPS):
            cols = slice(g * A_GROUP_DIM, (g + 1) * A_GROUP_DIM)
            ugs, vns = [], []
            for c in range(n_chunks):
                rows = slice(c * CHUNK, (c + 1) * CHUNK)
                ugs.append(_gelu(u_cu[rows, cols].astype(F32)))
                vg = _gelu(v_cu[rows, cols].astype(F32))
                mu = jnp.mean(vg, axis=-1, keepdims=True)
                vc = vg - mu
                var = jnp.mean(vc * vc, axis=-1, keepdims=True)
                vn = vc * lax.rsqrt(var + EPS) * lng_ref[:, cols] + lnb_ref[:, cols]
                vns.append(vn.astype(BF16))
                yield
            mixed = jnp.dot(ws_bf[g], jnp.concatenate(vns, axis=1), preferred_element_type=F32)
            for c in range(n_chunks):
                ag = ugs[c] * (mixed[:, c * A_GROUP_DIM:(c + 1) * A_GROUP_DIM] + bs_ref[g])
                a_buf[c, :, cols] = ag
                ss_a[c] = ss_a[c] + ag * ag
            yield
        for c in range(n_chunks):
            rows = slice(c * CHUNK, (c + 1) * CHUNK)
            scale_a = lax.rsqrt(jnp.sum(ss_a[c], axis=-1, keepdims=True) * (1.0 / A_WIDTH) + EPS)
            mix_nx[rows, 0:A_WIDTH] = (a_buf[c] * scale_a * ga_ref[...]).astype(BF16)
            yield

    def mix_b(c):
        rows = slice(c * CHUNK, (c + 1) * CHUNK)
        band = kvbuf[c * CHUNK:(c + 2) * CHUNK, :]
        k_stacks = _half_swapped_pair(band[:, :KV_WIDTH].astype(F32) * (HEAD_DIM ** -0.5 * LOG2E))
        v_stacks = _half_swapped_pair(band[:, KV_WIDTH:].astype(F32))
        chunk_id = (step - 1) * n_chunks + c
        first = (chunk_id % chunks_per_seq == 0).astype(jnp.int32)
        lane = lax.broadcasted_iota(jnp.int32, (CHUNK, 2 * HEAD_DIM), 1)
        lo = lane < HEAD_DIM
        srow = lax.broadcasted_iota(jnp.int32, (4 * CHUNK, 2 * HEAD_DIM), 0)
        slane = lax.broadcasted_iota(jnp.int32, (4 * CHUNK, 2 * HEAD_DIM), 1)
        row_sum_cols = ((srow < 2 * CHUNK) == (slane < HEAD_DIM)).astype(BF16)
        ss_b = jnp.zeros((CHUNK, 2 * HEAD_DIM), F32)
        for g in range(B_KV_HEADS):
            q_stack = jnp.concatenate(
                [q_cu[rows, (PAIRS_PER_KV * g + p) * 128:(PAIRS_PER_KV * g + p + 1) * 128]
                 for p in range(PAIRS_PER_KV)], axis=0)
            s_all = lax.dot_general(q_stack, k_stacks[g].astype(BF16),
                                    (((1,), (1,)), ((), ())),
                                    preferred_element_type=F32)
            yield
            p_rows = []
            sink_terms = []
            for p in range(PAIRS_PER_KV):
                p_cols = []
                sink_pair = []
                for par in range(2):
                    head = Q_PER_KV * g + 2 * p + par
                    rs = slice(p * CHUNK, (p + 1) * CHUNK)
                    cs = slice(par * 2 * CHUNK, (par + 1) * 2 * CHUNK)
                    s = s_all[rs, cs] + bias_ref[first, g, p, par]
                    sink = sinks_ref[head] * LOG2E
                    m = jnp.maximum(jnp.max(s, axis=-1, keepdims=True), sink)
                    p_cols.append(jnp.exp2(s - m).astype(BF16))
                    sink_pair.append(jnp.exp2(sink - m))
                p_rows.append(jnp.concatenate(p_cols, axis=1))
                sink_terms.append(sink_pair)
                yield
            probs = jnp.concatenate(p_rows, axis=0)
            values = jnp.concatenate([v_stacks[g].astype(BF16), row_sum_cols], axis=1)
            o_all = jnp.dot(probs, values, preferred_element_type=F32)
            for p in range(PAIRS_PER_KV):
                blk = PAIRS_PER_KV * g + p
                o_p = o_all[p * CHUNK:(p + 1) * CHUNK]
                den = o_p[:, 2 * HEAD_DIM:] + jnp.where(lo, sink_terms[p][0], sink_terms[p][1])
                o = o_p[:, :2 * HEAD_DIM] / den
                b_buf[c, :, blk * 128:(blk + 1) * 128] = o
                ss_b = ss_b + o * o
            yield
        scale_b = lax.rsqrt(jnp.sum(ss_b, axis=-1, keepdims=True) * (1.0 / B_WIDTH) + EPS)
        mix_nx[rows, A_WIDTH:MIX_WIDTH] = (b_buf[c] * scale_b * gb_ref[...]).astype(BF16)
        yield

    def mix_pieces():
        attention = (mix_b(c) for c in range(n_chunks))
        gens = [next(attention), mix_a()]
        while gens:
            for gen in list(gens):
                try:
                    next(gen)
                    yield
                except StopIteration:
                    following = next(attention, None) if gen is gens[0] else None
                    if following is None:
                        gens.remove(gen)
                    else:
                        gens[0] = following

    def output_pieces():
        mix_cu[...] = mix_nx[...]
        for c0 in range(0, D_MODEL, MXU_COLS):
            def piece(c0=c0):
                o_ref[:, c0:c0 + MXU_COLS] = x3_ref[:, c0:c0 + MXU_COLS] + jnp.dot(
                    mix_cu[...], wout_ref[:, c0:c0 + MXU_COLS], preferred_element_type=F32)
            yield piece

    def run():
        out_gen = output_pieces()
        next(out_gen)()
        matmuls = list(out_gen)
        take_projection()
        matmuls += list(project_pieces())
        cast_ffn_weights()
        n_yields = n_chunks * MIX_B_PIECES + A_GROUPS * (n_chunks + 1) + n_chunks
        spread = len(matmuls) - TRAILING_MATMULS
        done = 0
        for k, _ in enumerate(mix_pieces(), start=1):
            want = (k * spread) // n_yields
            while done < want:
                matmuls[done]()
                done += 1
        for piece in matmuls[done:]:
            piece()

    @pl.when(step == 0)
    def _():
        row = lax.broadcasted_iota(jnp.int32, (CHUNK, CHUNK), 0)
        col = lax.broadcasted_iota(jnp.int32, (CHUNK, CHUNK), 1)
        for g in range(A_GROUPS):
            ws_bf[g] = jnp.where(row >= col, ws_ref[g], 0.0).astype(BF16)
            diag = jnp.where(row == col, bsp_ref[g:g + 1, :], 0.0)
            bs_ref[g] = jnp.broadcast_to(jnp.sum(diag, axis=1, keepdims=True), (CHUNK, CHUNK))
        for ref in (kvbuf, u_nx, v_nx, q_nx, kv_nx, mix_nx):
            ref[...] = jnp.zeros_like(ref)
        load_projection_weights()

    run()


def _mix_block(x2, g, w_in, lng, lnb, w_s, b_s, sinks, table, bucket, ga, gb, w_out, w_up,
               w_down, layer, tq, seq):
    t = x2.shape[0]
    n_tiles = t // tq
    n_chunks = tq // CHUNK
    proj = lambda w: pltpu.VMEM((tq, w), BF16)
    out_tile = lambda s: (jnp.clip(s - 2, 0, n_tiles - 1), 0)
    up_rows, dn_rows = D_MODEL // n_tiles, D_FF // n_tiles
    slab = lambda s: (jnp.minimum(s, n_tiles - 1), 0)
    return pl.pallas_call(
        functools.partial(_mix_block_kernel, layer, seq // CHUNK),
        out_shape=[jax.ShapeDtypeStruct((t, D_MODEL), F32),
                   jax.ShapeDtypeStruct((D_MODEL, D_FF), BF16),
                   jax.ShapeDtypeStruct((D_FF, D_MODEL), BF16)],
        grid=(n_tiles + 2,),
        in_specs=[
            pl.BlockSpec((tq, D_MODEL), lambda s: (jnp.minimum(s, n_tiles - 1), 0)),
            pl.BlockSpec((tq, D_MODEL), out_tile),
            _const_spec((1, D_MODEL)),
            pl.BlockSpec(memory_space=pl.ANY),
            _const_spec((1, A_WIDTH)),
            _const_spec((1, A_WIDTH)),
            _const_spec(w_s.shape),
            _const_spec(b_s.shape),
            pl.BlockSpec(memory_space=pltpu.SMEM),
            pl.BlockSpec(memory_space=pltpu.SMEM),
            _const_spec(bucket.shape),
            _const_spec((1, A_WIDTH)),
            _const_spec((1, B_WIDTH)),
            pl.BlockSpec(memory_space=pl.ANY),
            pl.BlockSpec((None, up_rows, D_FF), lambda s: (layer,) + slab(s)),
            pl.BlockSpec((None, dn_rows, D_MODEL), lambda s: (layer,) + slab(s)),
        ],
        out_specs=[pl.BlockSpec((tq, D_MODEL), out_tile),
                   pl.BlockSpec((up_rows, D_FF), slab),
                   pl.BlockSpec((dn_rows, D_MODEL), slab)],
        scratch_shapes=[
            pltpu.VMEM((A_GROUPS, CHUNK, CHUNK), BF16),
            pltpu.VMEM((A_GROUPS, CHUNK, CHUNK), F32),
            pltpu.VMEM((2, B_KV_HEADS, PAIRS_PER_KV, 2, CHUNK, 2 * CHUNK), F32),
            pltpu.VMEM((D_MODEL, w_in.shape[2]), BF16),
            pltpu.VMEM((MIX_WIDTH, D_MODEL), BF16),
            pltpu.VMEM((2, WEIGHT_STAGE_ROWS, w_in.shape[2]), F32),
            pltpu.VMEM((2, WEIGHT_STAGE_ROWS, D_MODEL), F32),
            pltpu.SemaphoreType.DMA((2, 2)),
            proj(D_MODEL),
            proj(A_WIDTH), proj(A_WIDTH), proj(B_WIDTH), proj(2 * KV_WIDTH),
            proj(A_WIDTH), proj(A_WIDTH), proj(B_WIDTH),
            pltpu.VMEM((tq + CHUNK, 2 * KV_WIDTH), BF16),
            proj(MIX_WIDTH), proj(MIX_WIDTH),
            pltpu.VMEM((n_chunks, CHUNK, A_WIDTH), F32),
            pltpu.VMEM((n_chunks, CHUNK, B_WIDTH), F32),
        ],
        compiler_params=_params(1),
        name="mix_block",
    )(x2, x2, g, w_in, lng, lnb, w_s, b_s, sinks, table, bucket, ga, gb, w_out, w_up, w_down)


def _ffn_kernel(final_norm, h_hbm, g_ref, wu_ref, wd_ref, gf_ref, o_hbm,
                acc, nbuf, sem_in, sem_out):
    i = pl.program_id(0)
    f = pl.program_id(1)
    n_tiles = pl.num_programs(0)
    last_f = pl.num_programs(1) - 1
    tm = acc.shape[1]
    n_row_chunks = tm // FFN_NORM_ROWS
    slot = i % 2
    other = 1 - slot

    def h_copy(tile, s):
        return pltpu.make_async_copy(h_hbm.at[pl.ds(tile * tm, tm)], acc.at[s], sem_in)

    def o_copy(tile, s):
        return pltpu.make_async_copy(acc.at[s], o_hbm.at[pl.ds(tile * tm, tm)], sem_out.at[s])

    def chunk(k):
        return pl.ds(pl.multiple_of(k * FFN_NORM_ROWS, FFN_NORM_ROWS), FFN_NORM_ROWS)

    @pl.when(f == 0)
    def _():
        @pl.when(i == 0)
        def _():
            h_copy(0, 0).start()
        h_copy(i, slot).wait()

        def norm_rows(k, carry):
            hf = acc[slot, chunk(k), :]
            nbuf[chunk(k), :] = (hf * _rms_scale(hf) * g_ref[...]).astype(BF16)
            return carry
        lax.fori_loop(0, n_row_chunks, norm_rows, 0, unroll=True)

    @pl.when((f == 1) & (i + 1 < n_tiles))
    def _():
        @pl.when(i >= 1)
        def _():
            o_copy(i - 1, other).wait()
        h_copy(i + 1, other).start()

    z = jnp.maximum(jnp.dot(nbuf[...], wu_ref[...], preferred_element_type=F32), 0.0)
    update = jnp.dot((z * z).astype(BF16), wd_ref[...], preferred_element_type=F32)
    acc[slot] = acc[slot] + update

    @pl.when(f == last_f)
    def _():
        if final_norm:
            def final_rows(k, carry):
                y = acc[slot, chunk(k), :]
                acc[slot, chunk(k), :] = y * _rms_scale(y) * gf_ref[...]
                return carry
            lax.fori_loop(0, n_row_chunks, final_rows, 0, unroll=True)
        o_copy(i, slot).start()

        @pl.when(i == n_tiles - 1)
        def _():
            @pl.when(i >= 1)
            def _():
                o_copy(i - 1, other).wait()
            o_copy(i, slot).wait()


def _ffn(h, g, w_up, w_down, gf, final_norm, tm, tf):
    t = h.shape[0]
    assert D_FF // tf >= 2 and tm % FFN_NORM_ROWS == 0
    return pl.pallas_call(
        functools.partial(_ffn_kernel, final_norm),
        out_shape=jax.ShapeDtypeStruct((t, D_MODEL), F32),
        grid=(t // tm, D_FF // tf),
        in_specs=[
            pl.BlockSpec(memory_space=pl.ANY),
            _const_spec((1, D_MODEL)),
            pl.BlockSpec((D_MODEL, tf), lambda i, f: (0, f)),
            pl.BlockSpec((tf, D_MODEL), lambda i, f: (f, 0)),
            _const_spec((1, D_MODEL)),
        ],
        out_specs=pl.BlockSpec(memory_space=pl.ANY),
        scratch_shapes=[
            pltpu.VMEM((2, tm, D_MODEL), F32),
            pltpu.VMEM((tm, D_MODEL), BF16),
            pltpu.SemaphoreType.DMA(()),
            pltpu.SemaphoreType.DMA((2,)),
        ],
        compiler_params=_params(2),
        name="ffn",
    )(h, g, w_up, w_down, gf)


def _relative_buckets():
    i = jnp.arange(CHUNK)[:, None]
    j = jnp.arange(2 * CHUNK)[None, :]
    rel = i + CHUNK - j
    relc = jnp.maximum(rel, 0)
    n_exact = N_BUCKETS // 2
    relf = jnp.maximum(relc, n_exact).astype(F32)
    large = n_exact + (jnp.log(relf / n_exact) / math.log(MAX_DISTANCE / n_exact)
                       * (N_BUCKETS - n_exact)).astype(jnp.int32)
    large = jnp.minimum(large, N_BUCKETS - 1)
    bucket = jnp.where(relc < n_exact, relc, large)
    return jnp.where((rel >= 0) & (rel < CHUNK), bucket, -1).astype(jnp.int32)


def kernel(x, rel_bias_table, mix_norm_g, w_in, gate_norm_g, gate_norm_b, w_spatial, b_spatial,
           attn_sinks, out_norm_a_g, out_norm_b_g, w_out, ffn_norm_g, w_up, w_down, final_norm_g):
    bsz, seq, d = x.shape
    depth = w_in.shape[0]
    t = bsz * seq
    bucket = _relative_buckets()
    h = x.reshape(t, d)
    for layer in range(depth):
        h_mid, w_up_bf, w_down_bf = _mix_block(
            h, mix_norm_g[layer].reshape(1, d), w_in,
            gate_norm_g[layer].reshape(1, A_WIDTH), gate_norm_b[layer].reshape(1, A_WIDTH),
            w_spatial[layer], b_spatial[layer], attn_sinks[layer],
            rel_bias_table.astype(F32), bucket,
            out_norm_a_g[layer].reshape(1, A_WIDTH), out_norm_b_g[layer].reshape(1, B_WIDTH),
            w_out, w_up, w_down, layer=layer, tq=256, seq=seq)
        h = _ffn(h_mid, ffn_norm_g[layer].reshape(1, d), w_up_bf, w_down_bf,
                 final_norm_g.reshape(1, d), final_norm=layer == depth - 1, tm=1024, tf=2048)
    return h.reshape(bsz, seq, d)
```

```python
import functools
import math

import jax
import jax.numpy as jnp
from jax import lax
from jax.experimental import pallas as pl
from jax.experimental.pallas import tpu as pltpu

D_MODEL = 2048
CHUNK = 128
A_GROUPS = 8
A_GROUP_DIM = 128
A_WIDTH = A_GROUPS * A_GROUP_DIM
HEAD_DIM = 64
B_HEADS = 16
B_KV_HEADS = 2
Q_PER_KV = B_HEADS // B_KV_HEADS
PAIRS_PER_KV = Q_PER_KV // 2
B_WIDTH = B_HEADS * HEAD_DIM
KV_WIDTH = B_KV_HEADS * HEAD_DIM
N_BUCKETS = 32
MAX_DISTANCE = 128
MIX_WIDTH = A_WIDTH + B_WIDTH
D_FF = 4 * D_MODEL
EPS = 1e-5
NEG = -1e30
LOG2E = 1.4426950408889634
GELU_TANH_SCALE = math.sqrt(2.0 / math.pi)
GELU_CUBIC = 0.044715

V7X_VMEM_LIMIT_BYTES = 62 * 1024 * 1024
MXU_COLS = 256
FFN_NORM_ROWS = 128
TRAILING_MATMULS = 2
WEIGHT_STAGE_ROWS = 128
MIX_B_PIECES = B_KV_HEADS * (PAIRS_PER_KV + 2) + 1

BF16 = jnp.bfloat16
F32 = jnp.float32


def _rms_scale(xf):
    return lax.rsqrt(jnp.mean(xf * xf, axis=-1, keepdims=True) + EPS)


def _gelu(x):
    a = -2.0 * GELU_TANH_SCALE * LOG2E
    return x / (1.0 + jnp.exp2(x * (x * x * (a * GELU_CUBIC) + a)))


def _params(n_axes):
    return pltpu.CompilerParams(
        dimension_semantics=("arbitrary",) * n_axes,
        vmem_limit_bytes=V7X_VMEM_LIMIT_BYTES)


def _const_spec(shape):
    return pl.BlockSpec(shape, lambda *_: (0,) * len(shape), pipeline_mode=pl.Buffered(1))


def _half_swapped_pair(t):
    lane = lax.broadcasted_iota(jnp.int32, t.shape, 1)
    lo = lane < HEAD_DIM
    tr = pltpu.roll(t, HEAD_DIM, axis=1)
    zero = jnp.zeros_like(t)
    kv0 = jnp.concatenate([jnp.where(lo, t, zero), jnp.where(lo, zero, tr)], axis=0)
    kv1 = jnp.concatenate([jnp.where(lo, tr, zero), jnp.where(lo, zero, t)], axis=0)
    return kv0, kv1


def _mix_block_kernel(layer, chunks_per_seq,
                      x1_ref, x3_ref, g_ref, win_hbm, lng_ref, lnb_ref, ws_ref, bsp_ref,
                      sinks_ref, table_ref, bucket_ref, ga_ref, gb_ref, wout_hbm, wup_ref, wdn_ref,
                      o_ref, wup_bf_ref, wdn_bf_ref,
                      ws_bf, bs_ref, bias_ref, win_ref, wout_ref, stage_in, stage_out, stage_sem,
                      n_buf, u_nx, v_nx, q_nx, kv_nx, u_cu, v_cu, q_cu, kvbuf,
                      mix_nx, mix_cu, a_buf, b_buf):
    tq = x1_ref.shape[0]
    n_chunks = tq // CHUNK
    step = pl.program_id(0)

    def head_bias(h):
        bucket = bucket_ref[...]
        col = lax.broadcasted_iota(jnp.int32, bucket.shape, 1)
        val = jnp.full(bucket.shape, NEG, F32)
        for b in range(N_BUCKETS):
            val = jnp.where(bucket == b, table_ref[b, h] * LOG2E, val)
        g, p, par = h // Q_PER_KV, (h % Q_PER_KV) // 2, h % 2
        bias_ref[0, g, p, par] = val
        bias_ref[1, g, p, par] = jnp.where(col < CHUNK, NEG, val)

    def load_projection_weights():
        rows = stage_in.shape[1]
        n_slabs = D_MODEL // rows
        assert n_slabs == B_HEADS

        def copies(k, slot):
            src = pl.ds(k * rows, rows)
            return (pltpu.make_async_copy(win_hbm.at[layer, src], stage_in.at[slot],
                                          stage_sem.at[0, slot]),
                    pltpu.make_async_copy(wout_hbm.at[layer, src], stage_out.at[slot],
                                          stage_sem.at[1, slot]))

        for cp in copies(0, 0):
            cp.start()

        def slab(k, carry):
            slot = k % 2

            @pl.when(k + 1 < n_slabs)
            def _():
                for cp in copies(k + 1, 1 - slot):
                    cp.start()
            head_bias(k)
            for cp in copies(k, slot):
                cp.wait()
            dst = pl.ds(pl.multiple_of(k * rows, rows), rows)
            win_ref[dst, :] = stage_in[slot].astype(BF16)
            wout_ref[dst, :] = stage_out[slot].astype(BF16)
            return carry
        lax.fori_loop(0, n_slabs, slab, 0)

    def cast_ffn_weights():
        wup_bf_ref[...] = wup_ref[...].astype(BF16)
        wdn_bf_ref[...] = wdn_ref[...].astype(BF16)

    def project_pieces():
        xf = x1_ref[...]
        n_buf[...] = (xf * _rms_scale(xf) * g_ref[...]).astype(BF16)
        col = 0
        for dst in (u_nx, v_nx, q_nx, kv_nx):
            for c0 in range(0, dst.shape[1], MXU_COLS):
                def piece(dst=dst, c0=c0, col=col):
                    dst[:, c0:c0 + MXU_COLS] = jnp.dot(
                        n_buf[...], win_ref[:, col + c0:col + c0 + MXU_COLS],
                        preferred_element_type=F32).astype(BF16)
                yield piece
            col += dst.shape[1]

    def take_projection():
        kvbuf[0:CHUNK, :] = kvbuf[tq:tq + CHUNK, :]
        kvbuf[CHUNK:CHUNK + tq, :] = kv_nx[...]
        u_cu[...] = u_nx[...]
        v_cu[...] = v_nx[...]
        q_cu[...] = q_nx[...]

    def mix_a():
        ss_a = [jnp.zeros((CHUNK, A_GROUP_DIM), F32) for _ in range(n_chunks)]
        for g in range(A_GROUPS):
            cols = slice(g * A_GROUP_DIM, (g + 1) * A_GROUP_DIM)
            ugs, vns = [], []
            for c in range(n_chunks):
                rows = slice(c * CHUNK, (c + 1) * CHUNK)
                ugs.append(_gelu(u_cu[rows, cols].astype(F32)))
                vg = _gelu(v_cu[rows, cols].astype(F32))
                mu = jnp.mean(vg, axis=-1, keepdims=True)
                vc = vg - mu
                var = jnp.mean(vc * vc, axis=-1, keepdims=True)
                vn = vc * lax.rsqrt(var + EPS) * lng_ref[:, cols] + lnb_ref[:, cols]
                vns.append(vn.astype(BF16))
                yield
            mixed = jnp.dot(ws_bf[g], jnp.concatenate(vns, axis=1), preferred_element_type=F32)
            for c in range(n_chunks):
                ag = ugs[c] * (mixed[:, c * A_GROUP_DIM:(c + 1) * A_GROUP_DIM] + bs_ref[g])
                a_buf[c, :, cols] = ag
                ss_a[c] = ss_a[c] + ag * ag
            yield
        for c in range(n_chunks):
            rows = slice(c * CHUNK, (c + 1) * CHUNK)
            scale_a = lax.rsqrt(jnp.sum(ss_a[c], axis=-1, keepdims=True) * (1.0 / A_WIDTH) + EPS)
            mix_nx[rows, 0:A_WIDTH] = (a_buf[c] * scale_a * ga_ref[...]).astype(BF16)
            yield

    def mix_b(c):
        rows = slice(c * CHUNK, (c + 1) * CHUNK)
        band = kvbuf[c * CHUNK:(c + 2) * CHUNK, :]
        k_stacks = _half_swapped_pair(band[:, :KV_WIDTH].astype(F32) * (HEAD_DIM ** -0.5 * LOG2E))
        v_stacks = _half_swapped_pair(band[:, KV_WIDTH:].astype(F32))
        chunk_id = (step - 1) * n_chunks + c
        first = (chunk_id % chunks_per_seq == 0).astype(jnp.int32)
        lane = lax.broadcasted_iota(jnp.int32, (CHUNK, 2 * HEAD_DIM), 1)
        lo = lane < HEAD_DIM
        srow = lax.broadcasted_iota(jnp.int32, (4 * CHUNK, 2 * HEAD_DIM), 0)
        slane = lax.broadcasted_iota(jnp.int32, (4 * CHUNK, 2 * HEAD_DIM), 1)
        row_sum_cols = ((srow < 2 * CHUNK) == (slane < HEAD_DIM)).astype(BF16)
        ss_b = jnp.zeros((CHUNK, 2 * HEAD_DIM), F32)
        for g in range(B_KV_HEADS):
            q_stack = jnp.concatenate(
                [q_cu[rows, (PAIRS_PER_KV * g + p) * 128:(PAIRS_PER_KV * g + p + 1) * 128]
                 for p in range(PAIRS_PER_KV)], axis=0)
            s_all = lax.dot_general(q_stack, k_stacks[g].astype(BF16),
                                    (((1,), (1,)), ((), ())),
                                    preferred_element_type=F32)
            yield
            p_rows = []
            sink_terms = []
            for p in range(PAIRS_PER_KV):
                p_cols = []
                sink_pair = []
                for par in range(2):
                    head = Q_PER_KV * g + 2 * p + par
                    rs = slice(p * CHUNK, (p + 1) * CHUNK)
                    cs = slice(par * 2 * CHUNK, (par + 1) * 2 * CHUNK)
                    s = s_all[rs, cs] + bias_ref[first, g, p, par]
                    sink = sinks_ref[head] * LOG2E
                    m = jnp.maximum(jnp.max(s, axis=-1, keepdims=True), sink)
                    p_cols.append(jnp.exp2(s - m).astype(BF16))
                    sink_pair.append(jnp.exp2(sink - m))
                p_rows.append(jnp.concatenate(p_cols, axis=1))
                sink_terms.append(sink_pair)
                yield
            probs = jnp.concatenate(p_rows, axis=0)
            values = jnp.concatenate([v_stacks[g].astype(BF16), row_sum_cols], axis=1)
            o_all = jnp.dot(probs, values, preferred_element_type=F32)
            for p in range(PAIRS_PER_KV):
                blk = PAIRS_PER_KV * g + p
                o_p = o_all[p * CHUNK:(p + 1) * CHUNK]
                den = o_p[:, 2 * HEAD_DIM:] + jnp.where(lo, sink_terms[p][0], sink_terms[p][1])
                o = o_p[:, :2 * HEAD_DIM] / den
                b_buf[c, :, blk * 128:(blk + 1) * 128] = o
                ss_b = ss_b + o * o
            yield
        scale_b = lax.rsqrt(jnp.sum(ss_b, axis=-1, keepdims=True) * (1.0 / B_WIDTH) + EPS)
        mix_nx[rows, A_WIDTH:MIX_WIDTH] = (b_buf[c] * scale_b * gb_ref[...]).astype(BF16)
        yield

    def mix_pieces():
        attention = (mix_b(c) for c in range(n_chunks))
        gens = [next(attention), mix_a()]
        while gens:
            for gen in list(gens):
                try:
                    next(gen)
                    yield
                except StopIteration:
                    following = next(attention, None) if gen is gens[0] else None
                    if following is None:
                        gens.remove(gen)
                    else:
                        gens[0] = following

    def output_pieces():
        mix_cu[...] = mix_nx[...]
        for c0 in range(0, D_MODEL, MXU_COLS):
            def piece(c0=c0):
                o_ref[:, c0:c0 + MXU_COLS] = x3_ref[:, c0:c0 + MXU_COLS] + jnp.dot(
                    mix_cu[...], wout_ref[:, c0:c0 + MXU_COLS], preferred_element_type=F32)
            yield piece

    def run():
        out_gen = output_pieces()
        next(out_gen)()
        matmuls = list(out_gen)
        take_projection()
        matmuls += list(project_pieces())
        cast_ffn_weights()
        n_yields = n_chunks * MIX_B_PIECES + A_GROUPS * (n_chunks + 1) + n_chunks
        spread = len(matmuls) - TRAILING_MATMULS
        done = 0
        for k, _ in enumerate(mix_pieces(), start=1):
            want = (k * spread) // n_yields
            while done < want:
                matmuls[done]()
                done += 1
        for piece in matmuls[done:]:
            piece()

    @pl.when(step == 0)
    def _():
        row = lax.broadcasted_iota(jnp.int32, (CHUNK, CHUNK), 0)
        col = lax.broadcasted_iota(jnp.int32, (CHUNK, CHUNK), 1)
        for g in range(A_GROUPS):
            ws_bf[g] = jnp.where(row >= col, ws_ref[g], 0.0).astype(BF16)
            diag = jnp.where(row == col, bsp_ref[g:g + 1, :], 0.0)
            bs_ref[g] = jnp.broadcast_to(jnp.sum(diag, axis=1, keepdims=True), (CHUNK, CHUNK))
        for ref in (kvbuf, u_nx, v_nx, q_nx, kv_nx, mix_nx):
            ref[...] = jnp.zeros_like(ref)
        load_projection_weights()

    run()


def _mix_block(x2, g, w_in, lng, lnb, w_s, b_s, sinks, table, bucket, ga, gb, w_out, w_up,
               w_down, layer, tq, seq):
    t = x2.shape[0]
    n_tiles = t // tq
    n_chunks = tq // CHUNK
    proj = lambda w: pltpu.VMEM((tq, w), BF16)
    out_tile = lambda s: (jnp.clip(s - 2, 0, n_tiles - 1), 0)
    up_rows, dn_rows = D_MODEL // n_tiles, D_FF // n_tiles
    slab = lambda s: (jnp.minimum(s, n_tiles - 1), 0)
    return pl.pallas_call(
        functools.partial(_mix_block_kernel, layer, seq // CHUNK),
        out_shape=[jax.ShapeDtypeStruct((t, D_MODEL), F32),
                   jax.ShapeDtypeStruct((D_MODEL, D_FF), BF16),
                   jax.ShapeDtypeStruct((D_FF, D_MODEL), BF16)],
        grid=(n_tiles + 2,),
        in_specs=[
            pl.BlockSpec((tq, D_MODEL), lambda s: (jnp.minimum(s, n_tiles - 1), 0)),
            pl.BlockSpec((tq, D_MODEL), out_tile),
            _const_spec((1, D_MODEL)),
            pl.BlockSpec(memory_space=pl.ANY),
            _const_spec((1, A_WIDTH)),
            _const_spec((1, A_WIDTH)),
            _const_spec(w_s.shape),
            _const_spec(b_s.shape),
            pl.BlockSpec(memory_space=pltpu.SMEM),
            pl.BlockSpec(memory_space=pltpu.SMEM),
            _const_spec(bucket.shape),
            _const_spec((1, A_WIDTH)),
            _const_spec((1, B_WIDTH)),
            pl.BlockSpec(memory_space=pl.ANY),
            pl.BlockSpec((None, up_rows, D_FF), lambda s: (layer,) + slab(s)),
            pl.BlockSpec((None, dn_rows, D_MODEL), lambda s: (layer,) + slab(s)),
        ],
        out_specs=[pl.BlockSpec((tq, D_MODEL), out_tile),
                   pl.BlockSpec((up_rows, D_FF), slab),
                   pl.BlockSpec((dn_rows, D_MODEL), slab)],
        scratch_shapes=[
            pltpu.VMEM((A_GROUPS, CHUNK, CHUNK), BF16),
            pltpu.VMEM((A_GROUPS, CHUNK, CHUNK), F32),
            pltpu.VMEM((2, B_KV_HEADS, PAIRS_PER_KV, 2, CHUNK, 2 * CHUNK), F32),
            pltpu.VMEM((D_MODEL, w_in.shape[2]), BF16),
            pltpu.VMEM((MIX_WIDTH, D_MODEL), BF16),
            pltpu.VMEM((2, WEIGHT_STAGE_ROWS, w_in.shape[2]), F32),
            pltpu.VMEM((2, WEIGHT_STAGE_ROWS, D_MODEL), F32),
            pltpu.SemaphoreType.DMA((2, 2)),
            proj(D_MODEL),
            proj(A_WIDTH), proj(A_WIDTH), proj(B_WIDTH), proj(2 * KV_WIDTH),
            proj(A_WIDTH), proj(A_WIDTH), proj(B_WIDTH),
            pltpu.VMEM((tq + CHUNK, 2 * KV_WIDTH), BF16),
            proj(MIX_WIDTH), proj(MIX_WIDTH),
            pltpu.VMEM((n_chunks, CHUNK, A_WIDTH), F32),
            pltpu.VMEM((n_chunks, CHUNK, B_WIDTH), F32),
        ],
        compiler_params=_params(1),
        name="mix_block",
    )(x2, x2, g, w_in, lng, lnb, w_s, b_s, sinks, table, bucket, ga, gb, w_out, w_up, w_down)


def _ffn_kernel(final_norm, h_hbm, g_ref, wu_ref, wd_ref, gf_ref, o_hbm,
                acc, nbuf, sem_in, sem_out):
    i = pl.program_id(0)
    f = pl.program_id(1)
    n_tiles = pl.num_programs(0)
    last_f = pl.num_programs(1) - 1
    tm = acc.shape[1]
    n_row_chunks = tm // FFN_NORM_ROWS
    slot = i % 2
    other = 1 - slot

    def h_copy(tile, s):
        return pltpu.make_async_copy(h_hbm.at[pl.ds(tile * tm, tm)], acc.at[s], sem_in)

    def o_copy(tile, s):
        return pltpu.make_async_copy(acc.at[s], o_hbm.at[pl.ds(tile * tm, tm)], sem_out.at[s])

    def chunk(k):
        return pl.ds(pl.multiple_of(k * FFN_NORM_ROWS, FFN_NORM_ROWS), FFN_NORM_ROWS)

    @pl.when(f == 0)
    def _():
        @pl.when(i == 0)
        def _():
            h_copy(0, 0).start()
        h_copy(i, slot).wait()

        def norm_rows(k, carry):
            hf = acc[slot, chunk(k), :]
            nbuf[chunk(k), :] = (hf * _rms_scale(hf) * g_ref[...]).astype(BF16)
            return carry
        lax.fori_loop(0, n_row_chunks, norm_rows, 0, unroll=True)

    @pl.when((f == 1) & (i + 1 < n_tiles))
    def _():
        @pl.when(i >= 1)
        def _():
            o_copy(i - 1, other).wait()
        h_copy(i + 1, other).start()

    z = jnp.maximum(jnp.dot(nbuf[...], wu_ref[...], preferred_element_type=F32), 0.0)
    update = jnp.dot((z * z).astype(BF16), wd_ref[...], preferred_element_type=F32)
    acc[slot] = acc[slot] + update

    @pl.when(f == last_f)
    def _():
        if final_norm:
            def final_rows(k, carry):
                y = acc[slot, chunk(k), :]
                acc[slot, chunk(k), :] = y * _rms_scale(y) * gf_ref[...]
                return carry
            lax.fori_loop(0, n_row_chunks, final_rows, 0, unroll=True)
        o_copy(i, slot).start()

        @pl.when(i == n_tiles - 1)
        def _():
            @pl.when(i >= 1)
            def _():
                o_copy(i - 1, other).wait()
            o_copy(i, slot).wait()


def _ffn(h, g, w_up, w_down, gf, final_norm, tm, tf):
    t = h.shape[0]
    assert D_FF // tf >= 2 and tm % FFN_NORM_ROWS == 0
    return pl.pallas_call(
        functools.partial(_ffn_kernel, final_norm),
        out_shape=jax.ShapeDtypeStruct((t, D_MODEL), F32),
        grid=(t // tm, D_FF // tf),
        in_specs=[
            pl.BlockSpec(memory_space=pl.ANY),
            _const_spec((1, D_MODEL)),
            pl.BlockSpec((D_MODEL, tf), lambda i, f: (0, f)),
            pl.BlockSpec((tf, D_MODEL), lambda i, f: (f, 0)),
            _const_spec((1, D_MODEL)),
        ],
        out_specs=pl.BlockSpec(memory_space=pl.ANY),
        scratch_shapes=[
            pltpu.VMEM((2, tm, D_MODEL), F32),
            pltpu.VMEM((tm, D_MODEL), BF16),
            pltpu.SemaphoreType.DMA(()),
            pltpu.SemaphoreType.DMA((2,)),
        ],
        compiler_params=_params(2),
        name="ffn",
    )(h, g, w_up, w_down, gf)


def _relative_buckets():
    i = jnp.arange(CHUNK)[:, None]
    j = jnp.arange(2 * CHUNK)[None, :]
    rel = i + CHUNK - j
    relc = jnp.maximum(rel, 0)
    n_exact = N_BUCKETS // 2
    relf = jnp.maximum(relc, n_exact).astype(F32)
    large = n_exact + (jnp.log(relf / n_exact) / math.log(MAX_DISTANCE / n_exact)
                       * (N_BUCKETS - n_exact)).astype(jnp.int32)
    large = jnp.minimum(large, N_BUCKETS - 1)
    bucket = jnp.where(relc < n_exact, relc, large)
    return jnp.where((rel >= 0) & (rel < CHUNK), bucket, -1).astype(jnp.int32)


def kernel(x, rel_bias_table, mix_norm_g, w_in, gate_norm_g, gate_norm_b, w_spatial, b_spatial,
           attn_sinks, out_norm_a_g, out_norm_b_g, w_out, ffn_norm_g, w_up, w_down, final_norm_g):
    bsz, seq, d = x.shape
    depth = w_in.shape[0]
    t = bsz * seq
    tq, tm = 2 * CHUNK, 1024
    assert d == D_MODEL and x.dtype == F32, (x.shape, x.dtype)
    assert seq % tq == 0 and t % tm == 0, (bsz, seq)
    assert w_in.shape[1:] == (D_MODEL, 2 * A_WIDTH + B_WIDTH + 2 * KV_WIDTH), w_in.shape
    assert w_out.shape[1:] == (MIX_WIDTH, D_MODEL) and w_up.shape[1:] == (D_MODEL, D_FF)
    assert w_down.shape[1:] == (D_FF, D_MODEL) and w_spatial.shape[1:] == (A_GROUPS, CHUNK, CHUNK)
    assert rel_bias_table.shape == (N_BUCKETS, B_HEADS) and attn_sinks.shape[1:] == (B_HEADS,)
    bucket = _relative_buckets()
    h = x.reshape(t, d)
    for layer in range(depth):
        h_mid, w_up_bf, w_down_bf = _mix_block(
            h, mix_norm_g[layer].reshape(1, d), w_in,
            gate_norm_g[layer].reshape(1, A_WIDTH), gate_norm_b[layer].reshape(1, A_WIDTH),
            w_spatial[layer], b_spatial[layer], attn_sinks[layer],
            rel_bias_table.astype(F32), bucket,
            out_norm_a_g[layer].reshape(1, A_WIDTH), out_norm_b_g[layer].reshape(1, B_WIDTH),
            w_out, w_up, w_down, layer=layer, tq=tq, seq=seq)
        h = _ffn(h_mid, ffn_norm_g[layer].reshape(1, d), w_up_bf, w_down_bf,
                 final_norm_g.reshape(1, d), final_norm=layer == depth - 1, tm=tm, tf=2048)
    return h.reshape(bsz, seq, d)
```

```python
import functools
import math

import jax
import jax.numpy as jnp
from jax import lax
from jax.experimental import pallas as pl
from jax.experimental.pallas import tpu as pltpu

D_MODEL = 2048
CHUNK = 128
A_GROUPS = 8
A_GROUP_DIM = 128
A_WIDTH = A_GROUPS * A_GROUP_DIM
HEAD_DIM = 64
B_HEADS = 16
B_KV_HEADS = 2
Q_PER_KV = B_HEADS // B_KV_HEADS
PAIRS_PER_KV = Q_PER_KV // 2
B_WIDTH = B_HEADS * HEAD_DIM
KV_WIDTH = B_KV_HEADS * HEAD_DIM
N_BUCKETS = 32
MAX_DISTANCE = 128
MIX_WIDTH = A_WIDTH + B_WIDTH
D_FF = 4 * D_MODEL
EPS = 1e-5
NEG = -1e30
LOG2E = 1.4426950408889634
GELU_TANH_SCALE = math.sqrt(2.0 / math.pi)
GELU_CUBIC = 0.044715

V7X_VMEM_LIMIT_BYTES = 62 * 1024 * 1024
MXU_COLS = 256
FFN_NORM_ROWS = 128
TRAILING_MATMULS = 2
WEIGHT_STAGE_ROWS = 32
MIX_B_PIECES = B_KV_HEADS * (PAIRS_PER_KV + 2) + 1

BF16 = jnp.bfloat16
F32 = jnp.float32


def _rms_scale(xf):
    return lax.rsqrt(jnp.mean(xf * xf, axis=-1, keepdims=True) + EPS)


def _gelu(x):
    a = -2.0 * GELU_TANH_SCALE * LOG2E
    return x / (1.0 + jnp.exp2(x * (x * x * (a * GELU_CUBIC) + a)))


def _params(n_axes):
    return pltpu.CompilerParams(
        dimension_semantics=("arbitrary",) * n_axes,
        vmem_limit_bytes=V7X_VMEM_LIMIT_BYTES)


def _const_spec(shape):
    return pl.BlockSpec(shape, lambda *_: (0,) * len(shape), pipeline_mode=pl.Buffered(1))


def _half_swapped_pair(t):
    lane = lax.broadcasted_iota(jnp.int32, t.shape, 1)
    lo = lane < HEAD_DIM
    tr = pltpu.roll(t, HEAD_DIM, axis=1)
    zero = jnp.zeros_like(t)
    kv0 = jnp.concatenate([jnp.where(lo, t, zero), jnp.where(lo, zero, tr)], axis=0)
    kv1 = jnp.concatenate([jnp.where(lo, tr, zero), jnp.where(lo, zero, t)], axis=0)
    return kv0, kv1


def _mix_block_kernel(layer, chunks_per_seq,
                      x1_ref, x3_ref, g_ref, win_hbm, lng_ref, lnb_ref, ws_ref, bsp_ref,
                      sinks_ref, table_ref, bucket_ref, ga_ref, gb_ref, wout_hbm, wup_ref, wdn_ref,
                      o_ref, wup_bf_ref, wdn_bf_ref,
                      ws_bf, bs_ref, bias_ref, win_ref, wout_ref, stage_in, stage_out, stage_sem,
                      n_buf, u_nx, v_nx, q_nx, kv_nx, u_cu, v_cu, q_cu, kvbuf,
                      mix_nx, mix_cu, a_buf, b_buf):
    tq = x3_ref.shape[0]
    n_chunks = tq // CHUNK
    step = pl.program_id(0)

    def head_bias(h):
        bucket = bucket_ref[...]
        val = jnp.full(bucket.shape, NEG, F32)
        for b in range(N_BUCKETS):
            val = jnp.where(bucket == b, table_ref[b, h] * LOG2E, val)
        g, p, par = h // Q_PER_KV, (h % Q_PER_KV) // 2, h % 2
        bias_ref[g, p, par] = val

    def load_projection_weights():
        rows = stage_in.shape[1]
        n_slabs = D_MODEL // rows
        slabs_per_head = n_slabs // B_HEADS
        assert n_slabs == slabs_per_head * B_HEADS

        def copies(k, slot):
            src = pl.ds(k * rows, rows)
            return (pltpu.make_async_copy(win_hbm.at[layer, src], stage_in.at[slot],
                                          stage_sem.at[0, slot]),
                    pltpu.make_async_copy(wout_hbm.at[layer, src], stage_out.at[slot],
                                          stage_sem.at[1, slot]))

        for cp in copies(0, 0):
            cp.start()

        def slab(k, carry):
            slot = k % 2

            @pl.when(k + 1 < n_slabs)
            def _():
                for cp in copies(k + 1, 1 - slot):
                    cp.start()

            @pl.when(k % slabs_per_head == 0)
            def _():
                head_bias(k // slabs_per_head)
            for cp in copies(k, slot):
                cp.wait()
            dst = pl.ds(pl.multiple_of(k * rows, rows), rows)
            win_ref[dst, :] = stage_in[slot].astype(BF16)
            wout_ref[dst, :] = stage_out[slot].astype(BF16)
            return carry
        lax.fori_loop(0, n_slabs, slab, 0)

    def cast_ffn_weights():
        wup_bf_ref[...] = wup_ref[...].astype(BF16)
        wdn_bf_ref[...] = wdn_ref[...].astype(BF16)

    def project_pieces():
        xf = x1_ref[...]
        n_buf[...] = (xf * _rms_scale(xf) * g_ref[...]).astype(BF16)
        col = 0
        for dst in (u_nx, v_nx, q_nx, kv_nx):
            for c0 in range(0, dst.shape[1], MXU_COLS):
                def piece(dst=dst, c0=c0, col=col):
                    dst[:, c0:c0 + MXU_COLS] = jnp.dot(
                        n_buf[...], win_ref[:, col + c0:col + c0 + MXU_COLS],
                        preferred_element_type=F32).astype(BF16)
                yield piece
            col += dst.shape[1]

    def take_projection(half):
        rows = slice(half * tq, (half + 1) * tq)
        kvbuf[0:CHUNK, :] = kvbuf[tq:tq + CHUNK, :]
        kvbuf[CHUNK:CHUNK + tq, :] = kv_nx[rows, :]
        u_cu[...] = u_nx[rows, :]
        v_cu[...] = v_nx[rows, :]
        q_cu[...] = q_nx[rows, :]

    def mix_a():
        ss_a = [jnp.zeros((CHUNK, A_GROUP_DIM), F32) for _ in range(n_chunks)]
        for g in range(A_GROUPS):
            cols = slice(g * A_GROUP_DIM, (g + 1) * A_GROUP_DIM)
            ugs, vns = [], []
            for c in range(n_chunks):
                rows = slice(c * CHUNK, (c + 1) * CHUNK)
                ugs.append(_gelu(u_cu[rows, cols].astype(F32)))
                vg = _gelu(v_cu[rows, cols].astype(F32))
                mu = jnp.mean(vg, axis=-1, keepdims=True)
                vc = vg - mu
                var = jnp.mean(vc * vc, axis=-1, keepdims=True)
                vn = vc * lax.rsqrt(var + EPS) * lng_ref[:, cols] + lnb_ref[:, cols]
                vns.append(vn.astype(BF16))
                yield
            mixed = jnp.dot(ws_bf[g], jnp.concatenate(vns, axis=1), preferred_element_type=F32)
            for c in range(n_chunks):
                ag = ugs[c] * (mixed[:, c * A_GROUP_DIM:(c + 1) * A_GROUP_DIM] + bs_ref[g])
                a_buf[c, :, cols] = ag
                ss_a[c] = ss_a[c] + ag * ag
            yield
        for c in range(n_chunks):
            rows = slice(c * CHUNK, (c + 1) * CHUNK)
            scale_a = lax.rsqrt(jnp.sum(ss_a[c], axis=-1, keepdims=True) * (1.0 / A_WIDTH) + EPS)
            mix_nx[rows, 0:A_WIDTH] = (a_buf[c] * scale_a * ga_ref[...]).astype(BF16)
            yield

    def mix_b(c):
        rows = slice(c * CHUNK, (c + 1) * CHUNK)
        band = kvbuf[c * CHUNK:(c + 2) * CHUNK, :]
        k_stacks = _half_swapped_pair(band[:, :KV_WIDTH].astype(F32) * (HEAD_DIM ** -0.5 * LOG2E))
        v_stacks = _half_swapped_pair(band[:, KV_WIDTH:].astype(F32))
        chunk_id = (step - 1) * n_chunks + c
        key = lax.broadcasted_iota(jnp.int32, (CHUNK, 2 * CHUNK), 1)
        no_prev = jnp.logical_and(key < CHUNK, chunk_id % chunks_per_seq == 0)
        lane = lax.broadcasted_iota(jnp.int32, (CHUNK, 2 * HEAD_DIM), 1)
        lo = lane < HEAD_DIM
        srow = lax.broadcasted_iota(jnp.int32, (4 * CHUNK, 2 * HEAD_DIM), 0)
        slane = lax.broadcasted_iota(jnp.int32, (4 * CHUNK, 2 * HEAD_DIM), 1)
        row_sum_cols = ((srow < 2 * CHUNK) == (slane < HEAD_DIM)).astype(BF16)
        ss_b = jnp.zeros((CHUNK, 2 * HEAD_DIM), F32)
        for g in range(B_KV_HEADS):
            q_stack = jnp.concatenate(
                [q_cu[rows, (PAIRS_PER_KV * g + p) * 128:(PAIRS_PER_KV * g + p + 1) * 128]
                 for p in range(PAIRS_PER_KV)], axis=0)
            s_all = lax.dot_general(q_stack, k_stacks[g].astype(BF16),
                                    (((1,), (1,)), ((), ())),
                                    preferred_element_type=F32)
            yield
            p_rows = []
            sink_terms = []
            for p in range(PAIRS_PER_KV):
                p_cols = []
                sink_pair = []
                for par in range(2):
                    head = Q_PER_KV * g + 2 * p + par
                    rs = slice(p * CHUNK, (p + 1) * CHUNK)
                    cs = slice(par * 2 * CHUNK, (par + 1) * 2 * CHUNK)
                    s = jnp.where(no_prev, NEG, s_all[rs, cs] + bias_ref[g, p, par])
                    sink = sinks_ref[head] * LOG2E
                    m = jnp.maximum(jnp.max(s, axis=-1, keepdims=True), sink)
                    p_cols.append(jnp.exp2(s - m).astype(BF16))
                    sink_pair.append(jnp.exp2(sink - m))
                p_rows.append(jnp.concatenate(p_cols, axis=1))
                sink_terms.append(sink_pair)
                yield
            probs = jnp.concatenate(p_rows, axis=0)
            values = jnp.concatenate([v_stacks[g].astype(BF16), row_sum_cols], axis=1)
            o_all = jnp.dot(probs, values, preferred_element_type=F32)
            for p in range(PAIRS_PER_KV):
                blk = PAIRS_PER_KV * g + p
                o_p = o_all[p * CHUNK:(p + 1) * CHUNK]
                den = o_p[:, 2 * HEAD_DIM:] + jnp.where(lo, sink_terms[p][0], sink_terms[p][1])
                o = o_p[:, :2 * HEAD_DIM] / den
                b_buf[c, :, blk * 128:(blk + 1) * 128] = o
                ss_b = ss_b + o * o
            yield
        scale_b = lax.rsqrt(jnp.sum(ss_b, axis=-1, keepdims=True) * (1.0 / B_WIDTH) + EPS)
        mix_nx[rows, A_WIDTH:MIX_WIDTH] = (b_buf[c] * scale_b * gb_ref[...]).astype(BF16)
        yield

    def mix_pieces():
        attention = (mix_b(c) for c in range(n_chunks))
        gens = [next(attention), mix_a()]
        while gens:
            for gen in list(gens):
                try:
                    next(gen)
                    yield
                except StopIteration:
                    following = next(attention, None) if gen is gens[0] else None
                    if following is None:
                        gens.remove(gen)
                    else:
                        gens[0] = following

    def output_pieces():
        mix_cu[...] = mix_nx[...]
        for c0 in range(0, D_MODEL, MXU_COLS):
            def piece(c0=c0):
                o_ref[:, c0:c0 + MXU_COLS] = x3_ref[:, c0:c0 + MXU_COLS] + jnp.dot(
                    mix_cu[...], wout_ref[:, c0:c0 + MXU_COLS], preferred_element_type=F32)
            yield piece

    def run(project):
        out_gen = output_pieces()
        next(out_gen)()
        matmuls = list(out_gen)
        take_projection(1 if project else 0)
        if project:
            matmuls += list(project_pieces())
        cast_ffn_weights()
        n_yields = n_chunks * MIX_B_PIECES + A_GROUPS * (n_chunks + 1) + n_chunks
        spread = len(matmuls) - TRAILING_MATMULS
        done = 0
        for k, _ in enumerate(mix_pieces(), start=1):
            want = (k * spread) // n_yields
            while done < want:
                matmuls[done]()
                done += 1
        for piece in matmuls[done:]:
            piece()

    @pl.when(step == 0)
    def _():
        row = lax.broadcasted_iota(jnp.int32, (CHUNK, CHUNK), 0)
        col = lax.broadcasted_iota(jnp.int32, (CHUNK, CHUNK), 1)
        for g in range(A_GROUPS):
            ws_bf[g] = jnp.where(row >= col, ws_ref[g], 0.0).astype(BF16)
            diag = jnp.where(row == col, bsp_ref[g:g + 1, :], 0.0)
            bs_ref[g] = jnp.broadcast_to(jnp.sum(diag, axis=1, keepdims=True), (CHUNK, CHUNK))
        for ref in (kvbuf, u_nx, v_nx, q_nx, kv_nx, mix_nx):
            ref[...] = jnp.zeros_like(ref)
        load_projection_weights()

    @pl.when(step % 2 == 0)
    def _():
        run(True)

    @pl.when(step % 2 == 1)
    def _():
        run(False)


def _mix_block(x2, g, w_in, lng, lnb, w_s, b_s, sinks, table, bucket, ga, gb, w_out, w_up,
               w_down, layer, tq, seq):
    t = x2.shape[0]
    n_tiles = t // tq
    n_chunks = tq // CHUNK
    assert n_tiles % 2 == 0
    proj = lambda w: pltpu.VMEM((tq, w), BF16)
    proj2 = lambda w: pltpu.VMEM((2 * tq, w), BF16)
    out_tile = lambda s: (jnp.clip(s - 2, 0, n_tiles - 1), 0)
    up_rows, dn_rows = D_MODEL // n_tiles, D_FF // n_tiles
    slab = lambda s: (jnp.minimum(s, n_tiles - 1), 0)
    return pl.pallas_call(
        functools.partial(_mix_block_kernel, layer, seq // CHUNK),
        out_shape=[jax.ShapeDtypeStruct((t, D_MODEL), F32),
                   jax.ShapeDtypeStruct((D_MODEL, D_FF), BF16),
                   jax.ShapeDtypeStruct((D_FF, D_MODEL), BF16)],
        grid=(n_tiles + 2,),
        in_specs=[
            pl.BlockSpec((2 * tq, D_MODEL), lambda s: (jnp.minimum(s // 2, n_tiles // 2 - 1), 0)),
            pl.BlockSpec((tq, D_MODEL), out_tile),
            _const_spec((1, D_MODEL)),
            pl.BlockSpec(memory_space=pl.ANY),
            _const_spec((1, A_WIDTH)),
            _const_spec((1, A_WIDTH)),
            _const_spec(w_s.shape),
            _const_spec(b_s.shape),
            pl.BlockSpec(memory_space=pltpu.SMEM),
            pl.BlockSpec(memory_space=pltpu.SMEM),
            _const_spec(bucket.shape),
            _const_spec((1, A_WIDTH)),
            _const_spec((1, B_WIDTH)),
            pl.BlockSpec(memory_space=pl.ANY),
            pl.BlockSpec((None, up_rows, D_FF), lambda s: (layer,) + slab(s)),
            pl.BlockSpec((None, dn_rows, D_MODEL), lambda s: (layer,) + slab(s)),
        ],
        out_specs=[pl.BlockSpec((tq, D_MODEL), out_tile),
                   pl.BlockSpec((up_rows, D_FF), slab),
                   pl.BlockSpec((dn_rows, D_MODEL), slab)],
        scratch_shapes=[
            pltpu.VMEM((A_GROUPS, CHUNK, CHUNK), BF16),
            pltpu.VMEM((A_GROUPS, CHUNK, CHUNK), F32),
            pltpu.VMEM((B_KV_HEADS, PAIRS_PER_KV, 2, CHUNK, 2 * CHUNK), F32),
            pltpu.VMEM((D_MODEL, w_in.shape[2]), BF16),
            pltpu.VMEM((MIX_WIDTH, D_MODEL), BF16),
            pltpu.VMEM((2, WEIGHT_STAGE_ROWS, w_in.shape[2]), F32),
            pltpu.VMEM((2, WEIGHT_STAGE_ROWS, D_MODEL), F32),
            pltpu.SemaphoreType.DMA((2, 2)),
            proj2(D_MODEL),
            proj2(A_WIDTH), proj2(A_WIDTH), proj2(B_WIDTH), proj2(2 * KV_WIDTH),
            proj(A_WIDTH), proj(A_WIDTH), proj(B_WIDTH),
            pltpu.VMEM((tq + CHUNK, 2 * KV_WIDTH), BF16),
            proj(MIX_WIDTH), proj(MIX_WIDTH),
            pltpu.VMEM((n_chunks, CHUNK, A_WIDTH), F32),
            pltpu.VMEM((n_chunks, CHUNK, B_WIDTH), F32),
        ],
        compiler_params=_params(1),
        name="mix_block",
    )(x2, x2, g, w_in, lng, lnb, w_s, b_s, sinks, table, bucket, ga, gb, w_out, w_up, w_down)


def _ffn_kernel(final_norm, h_hbm, g_ref, wu_ref, wd_ref, gf_ref, o_hbm,
                acc, nbuf, sem_in, sem_out):
    i = pl.program_id(0)
    f = pl.program_id(1)
    n_tiles = pl.num_programs(0)
    last_f = pl.num_programs(1) - 1
    tm = acc.shape[1]
    n_row_chunks = tm // FFN_NORM_ROWS
    slot = i % 2
    other = 1 - slot

    def h_copy(tile, s):
        return pltpu.make_async_copy(h_hbm.at[pl.ds(tile * tm, tm)], acc.at[s], sem_in)

    def o_copy(tile, s):
        return pltpu.make_async_copy(acc.at[s], o_hbm.at[pl.ds(tile * tm, tm)], sem_out.at[s])

    def chunk(k):
        return pl.ds(pl.multiple_of(k * FFN_NORM_ROWS, FFN_NORM_ROWS), FFN_NORM_ROWS)

    @pl.when(f == 0)
    def _():
        @pl.when(i == 0)
        def _():
            h_copy(0, 0).start()
        h_copy(i, slot).wait()

        def norm_rows(k, carry):
            hf = acc[slot, chunk(k), :]
            nbuf[chunk(k), :] = (hf * _rms_scale(hf) * g_ref[...]).astype(BF16)
            return carry
        lax.fori_loop(0, n_row_chunks, norm_rows, 0, unroll=True)

    @pl.when((f == 1) & (i + 1 < n_tiles))
    def _():
        @pl.when(i >= 1)
        def _():
            o_copy(i - 1, other).wait()
        h_copy(i + 1, other).start()

    z = jnp.maximum(jnp.dot(nbuf[...], wu_ref[...], preferred_element_type=F32), 0.0)
    update = jnp.dot((z * z).astype(BF16), wd_ref[...], preferred_element_type=F32)
    acc[slot] = acc[slot] + update

    @pl.when(f == last_f)
    def _():
        if final_norm:
            def final_rows(k, carry):
                y = acc[slot, chunk(k), :]
                acc[slot, chunk(k), :] = y * _rms_scale(y) * gf_ref[...]
                return carry
            lax.fori_loop(0, n_row_chunks, final_rows, 0, unroll=True)
        o_copy(i, slot).start()

        @pl.when(i == n_tiles - 1)
        def _():
            @pl.when(i >= 1)
            def _():
                o_copy(i - 1, other).wait()
            o_copy(i, slot).wait()


def _ffn(h, g, w_up, w_down, gf, final_norm, tm, tf):
    t = h.shape[0]
    assert D_FF // tf >= 2 and tm % FFN_NORM_ROWS == 0
    return pl.pallas_call(
        functools.partial(_ffn_kernel, final_norm),
        out_shape=jax.ShapeDtypeStruct((t, D_MODEL), F32),
        grid=(t // tm, D_FF // tf),
        in_specs=[
            pl.BlockSpec(memory_space=pl.ANY),
            _const_spec((1, D_MODEL)),
            pl.BlockSpec((D_MODEL, tf), lambda i, f: (0, f)),
            pl.BlockSpec((tf, D_MODEL), lambda i, f: (f, 0)),
            _const_spec((1, D_MODEL)),
        ],
        out_specs=pl.BlockSpec(memory_space=pl.ANY),
        scratch_shapes=[
            pltpu.VMEM((2, tm, D_MODEL), F32),
            pltpu.VMEM((tm, D_MODEL), BF16),
            pltpu.SemaphoreType.DMA(()),
            pltpu.SemaphoreType.DMA((2,)),
        ],
        compiler_params=_params(2),
        name="ffn",
    )(h, g, w_up, w_down, gf)


def _relative_buckets():
    i = jnp.arange(CHUNK)[:, None]
    j = jnp.arange(2 * CHUNK)[None, :]
    rel = i + CHUNK - j
    relc = jnp.maximum(rel, 0)
    n_exact = N_BUCKETS // 2
    relf = jnp.maximum(relc, n_exact).astype(F32)
    large = n_exact + (jnp.log(relf / n_exact) / math.log(MAX_DISTANCE / n_exact)
                       * (N_BUCKETS - n_exact)).astype(jnp.int32)
    large = jnp.minimum(large, N_BUCKETS - 1)
    bucket = jnp.where(relc < n_exact, relc, large)
    return jnp.where((rel >= 0) & (rel < CHUNK), bucket, -1).astype(jnp.int32)


def kernel(x, rel_bias_table, mix_norm_g, w_in, gate_norm_g, gate_norm_b, w_spatial, b_spatial,
           attn_sinks, out_norm_a_g, out_norm_b_g, w_out, ffn_norm_g, w_up, w_down, final_norm_g):
    bsz, seq, d = x.shape
    depth = w_in.shape[0]
    t = bsz * seq
    tq, tm = 2 * CHUNK, 1024
    assert d == D_MODEL and x.dtype == F32, (x.shape, x.dtype)
    assert seq % tq == 0 and t % tm == 0, (bsz, seq)
    assert w_in.shape[1:] == (D_MODEL, 2 * A_WIDTH + B_WIDTH + 2 * KV_WIDTH), w_in.shape
    assert w_out.shape[1:] == (MIX_WIDTH, D_MODEL) and w_up.shape[1:] == (D_MODEL, D_FF)
    assert w_down.shape[1:] == (D_FF, D_MODEL) and w_spatial.shape[1:] == (A_GROUPS, CHUNK, CHUNK)
    assert rel_bias_table.shape == (N_BUCKETS, B_HEADS) and attn_sinks.shape[1:] == (B_HEADS,)
    bucket = _relative_buckets()
    h = x.reshape(t, d)
    for layer in range(depth):
        h_mid, w_up_bf, w_down_bf = _mix_block(
            h, mix_norm_g[layer].reshape(1, d), w_in,
            gate_norm_g[layer].reshape(1, A_WIDTH), gate_norm_b[layer].reshape(1, A_WIDTH),
            w_spatial[layer], b_spatial[layer], attn_sinks[layer],
            rel_bias_table.astype(F32), bucket,
            out_norm_a_g[layer].reshape(1, A_WIDTH), out_norm_b_g[layer].reshape(1, B_WIDTH),
            w_out, w_up, w_down, layer=layer, tq=tq, seq=seq)
        h = _ffn(h_mid, ffn_norm_g[layer].reshape(1, d), w_up_bf, w_down_bf,
                 final_norm_g.reshape(1, d), final_norm=layer == depth - 1, tm=tm, tf=2048)
    return h.reshape(bsz, seq, d)
```

```python
import functools
import math

import jax
import jax.numpy as jnp
from jax import lax
from jax.experimental import pallas as pl
from jax.experimental.pallas import tpu as pltpu

D_MODEL = 2048
CHUNK = 128
A_GROUPS = 8
A_GROUP_DIM = 128
A_WIDTH = A_GROUPS * A_GROUP_DIM
HEAD_DIM = 64
B_HEADS = 16
B_KV_HEADS = 2
Q_PER_KV = B_HEADS // B_KV_HEADS
PAIRS_PER_KV = Q_PER_KV // 2
B_WIDTH = B_HEADS * HEAD_DIM
KV_WIDTH = B_KV_HEADS * HEAD_DIM
N_BUCKETS = 32
MAX_DISTANCE = 128
MIX_WIDTH = A_WIDTH + B_WIDTH
D_FF = 4 * D_MODEL
EPS = 1e-5
NEG = -1e30
LOG2E = 1.4426950408889634
GELU_TANH_SCALE = math.sqrt(2.0 / math.pi)
GELU_CUBIC = 0.044715

V7X_VMEM_LIMIT_BYTES = 62 * 1024 * 1024
MXU_COLS = 256
FFN_NORM_ROWS = 128
TRAILING_MATMULS = 2
WEIGHT_STAGE_ROWS = 128
X_RING_SLOTS = 4
MIX_B_PIECES = B_KV_HEADS * (PAIRS_PER_KV + 2) + 1

BF16 = jnp.bfloat16
F32 = jnp.float32


def _rms_scale(xf):
    return lax.rsqrt(jnp.mean(xf * xf, axis=-1, keepdims=True) + EPS)


def _gelu(x):
    a = -2.0 * GELU_TANH_SCALE * LOG2E
    return x / (1.0 + jnp.exp2(x * (x * x * (a * GELU_CUBIC) + a)))


def _params(n_axes):
    return pltpu.CompilerParams(
        dimension_semantics=("arbitrary",) * n_axes,
        vmem_limit_bytes=V7X_VMEM_LIMIT_BYTES)


def _const_spec(shape):
    return pl.BlockSpec(shape, lambda *_: (0,) * len(shape), pipeline_mode=pl.Buffered(1))


def _half_swapped_pair(t):
    lane = lax.broadcasted_iota(jnp.int32, t.shape, 1)
    lo = lane < HEAD_DIM
    tr = pltpu.roll(t, HEAD_DIM, axis=1)
    zero = jnp.zeros_like(t)
    kv0 = jnp.concatenate([jnp.where(lo, t, zero), jnp.where(lo, zero, tr)], axis=0)
    kv1 = jnp.concatenate([jnp.where(lo, tr, zero), jnp.where(lo, zero, t)], axis=0)
    return kv0, kv1


def _mix_block_kernel(layer, chunks_per_seq,
                      x_hbm, g_ref, win_hbm, lng_ref, lnb_ref, ws_ref, bsp_ref,
                      sinks_ref, table_ref, bucket_ref, ga_ref, gb_ref, wout_hbm, wup_ref, wdn_ref,
                      o_ref, wup_bf_ref, wdn_bf_ref,
                      ws_bf, bs_ref, bias_ref, win_ref, wout_ref, stage_in, stage_out, stage_sem,
                      xring, x_sem,
                      n_buf, u_nx, v_nx, q_nx, kv_nx, u_cu, v_cu, q_cu, kvbuf,
                      mix_nx, mix_cu, a_buf, b_buf):
    tq = xring.shape[1]
    n_chunks = tq // CHUNK
    step = pl.program_id(0)
    n_tiles = pl.num_programs(0) - 2

    def x_copy(tile):
        slot = tile % X_RING_SLOTS
        return pltpu.make_async_copy(x_hbm.at[pl.ds(pl.multiple_of(tile * tq, tq), tq)],
                                     xring.at[slot], x_sem.at[slot])

    def head_bias(h):
        bucket = bucket_ref[...]
        col = lax.broadcasted_iota(jnp.int32, bucket.shape, 1)
        val = jnp.full(bucket.shape, NEG, F32)
        for b in range(N_BUCKETS):
            val = jnp.where(bucket == b, table_ref[b, h] * LOG2E, val)
        g, p, par = h // Q_PER_KV, (h % Q_PER_KV) // 2, h % 2
        bias_ref[0, g, p, par] = val
        bias_ref[1, g, p, par] = jnp.where(col < CHUNK, NEG, val)

    def load_projection_weights():
        rows = stage_in.shape[1]
        n_slabs = D_MODEL // rows
        assert n_slabs == B_HEADS

        def copies(k, slot):
            src = pl.ds(k * rows, rows)
            return (pltpu.make_async_copy(win_hbm.at[layer, src], stage_in.at[slot],
                                          stage_sem.at[0, slot]),
                    pltpu.make_async_copy(wout_hbm.at[layer, src], stage_out.at[slot],
                                          stage_sem.at[1, slot]))

        for cp in copies(0, 0):
            cp.start()

        def slab(k, carry):
            slot = k % 2

            @pl.when(k + 1 < n_slabs)
            def _():
                for cp in copies(k + 1, 1 - slot):
                    cp.start()
            head_bias(k)
            for cp in copies(k, slot):
                cp.wait()
            dst = pl.ds(pl.multiple_of(k * rows, rows), rows)
            win_ref[dst, :] = stage_in[slot].astype(BF16)
            wout_ref[dst, :] = stage_out[slot].astype(BF16)
            return carry
        lax.fori_loop(0, n_slabs, slab, 0)

    def cast_ffn_weights():
        wup_bf_ref[...] = wup_ref[...].astype(BF16)
        wdn_bf_ref[...] = wdn_ref[...].astype(BF16)

    def project_pieces():
        xf = xring[step % X_RING_SLOTS]
        n_buf[...] = (xf * _rms_scale(xf) * g_ref[...]).astype(BF16)
        col = 0
        for dst in (u_nx, v_nx, q_nx, kv_nx):
            for c0 in range(0, dst.shape[1], MXU_COLS):
                def piece(dst=dst, c0=c0, col=col):
                    dst[:, c0:c0 + MXU_COLS] = jnp.dot(
                        n_buf[...], win_ref[:, col + c0:col + c0 + MXU_COLS],
                        preferred_element_type=F32).astype(BF16)
                yield piece
            col += dst.shape[1]

    def take_projection():
        kvbuf[0:CHUNK, :] = kvbuf[tq:tq + CHUNK, :]
        kvbuf[CHUNK:CHUNK + tq, :] = kv_nx[...]
        u_cu[...] = u_nx[...]
        v_cu[...] = v_nx[...]
        q_cu[...] = q_nx[...]

    def mix_a():
        ss_a = [jnp.zeros((CHUNK, A_GROUP_DIM), F32) for _ in range(n_chunks)]
        for g in range(A_GROUPS):
            cols = slice(g * A_GROUP_DIM, (g + 1) * A_GROUP_DIM)
            ugs, vns = [], []
            for c in range(n_chunks):
                rows = slice(c * CHUNK, (c + 1) * CHUNK)
                ugs.append(_gelu(u_cu[rows, cols].astype(F32)))
                vg = _gelu(v_cu[rows, cols].astype(F32))
                mu = jnp.mean(vg, axis=-1, keepdims=True)
                vc = vg - mu
                var = jnp.mean(vc * vc, axis=-1, keepdims=True)
                vn = vc * lax.rsqrt(var + EPS) * lng_ref[:, cols] + lnb_ref[:, cols]
                vns.append(vn.astype(BF16))
                yield
            mixed = jnp.dot(ws_bf[g], jnp.concatenate(vns, axis=1), preferred_element_type=F32)
            for c in range(n_chunks):
                ag = ugs[c] * (mixed[:, c * A_GROUP_DIM:(c + 1) * A_GROUP_DIM] + bs_ref[g])
                a_buf[c, :, cols] = ag
                ss_a[c] = ss_a[c] + ag * ag
            yield
        for c in range(n_chunks):
            rows = slice(c * CHUNK, (c + 1) * CHUNK)
            scale_a = lax.rsqrt(jnp.sum(ss_a[c], axis=-1, keepdims=True) * (1.0 / A_WIDTH) + EPS)
            mix_nx[rows, 0:A_WIDTH] = (a_buf[c] * scale_a * ga_ref[...]).astype(BF16)
            yield

    def mix_b(c):
        rows = slice(c * CHUNK, (c + 1) * CHUNK)
        band = kvbuf[c * CHUNK:(c + 2) * CHUNK, :]
        k_stacks = _half_swapped_pair(band[:, :KV_WIDTH].astype(F32) * (HEAD_DIM ** -0.5 * LOG2E))
        v_stacks = _half_swapped_pair(band[:, KV_WIDTH:].astype(F32))
        chunk_id = (step - 1) * n_chunks + c
        first = (chunk_id % chunks_per_seq == 0).astype(jnp.int32)
        lane = lax.broadcasted_iota(jnp.int32, (CHUNK, 2 * HEAD_DIM), 1)
        lo = lane < HEAD_DIM
        srow = lax.broadcasted_iota(jnp.int32, (4 * CHUNK, 2 * HEAD_DIM), 0)
        slane = lax.broadcasted_iota(jnp.int32, (4 * CHUNK, 2 * HEAD_DIM), 1)
        row_sum_cols = ((srow < 2 * CHUNK) == (slane < HEAD_DIM)).astype(BF16)
        ss_b = jnp.zeros((CHUNK, 2 * HEAD_DIM), F32)
        for g in range(B_KV_HEADS):
            q_stack = jnp.concatenate(
                [q_cu[rows, (PAIRS_PER_KV * g + p) * 128:(PAIRS_PER_KV * g + p + 1) * 128]
                 for p in range(PAIRS_PER_KV)], axis=0)
            s_all = lax.dot_general(q_stack, k_stacks[g].astype(BF16),
                                    (((1,), (1,)), ((), ())),
                                    preferred_element_type=F32)
            yield
            p_rows = []
            sink_terms = []
            for p in range(PAIRS_PER_KV):
                p_cols = []
                sink_pair = []
                for par in range(2):
                    head = Q_PER_KV * g + 2 * p + par
                    rs = slice(p * CHUNK, (p + 1) * CHUNK)
                    cs = slice(par * 2 * CHUNK, (par + 1) * 2 * CHUNK)
                    s = s_all[rs, cs] + bias_ref[first, g, p, par]
                    sink = sinks_ref[head] * LOG2E
                    m = jnp.maximum(jnp.max(s, axis=-1, keepdims=True), sink)
                    p_cols.append(jnp.exp2(s - m).astype(BF16))
                    sink_pair.append(jnp.exp2(sink - m))
                p_rows.append(jnp.concatenate(p_cols, axis=1))
                sink_terms.append(sink_pair)
                yield
            probs = jnp.concatenate(p_rows, axis=0)
            values = jnp.concatenate([v_stacks[g].astype(BF16), row_sum_cols], axis=1)
            o_all = jnp.dot(probs, values, preferred_element_type=F32)
            for p in range(PAIRS_PER_KV):
                blk = PAIRS_PER_KV * g + p
                o_p = o_all[p * CHUNK:(p + 1) * CHUNK]
                den = o_p[:, 2 * HEAD_DIM:] + jnp.where(lo, sink_terms[p][0], sink_terms[p][1])
                o = o_p[:, :2 * HEAD_DIM] / den
                b_buf[c, :, blk * 128:(blk + 1) * 128] = o
                ss_b = ss_b + o * o
            yield
        scale_b = lax.rsqrt(jnp.sum(ss_b, axis=-1, keepdims=True) * (1.0 / B_WIDTH) + EPS)
        mix_nx[rows, A_WIDTH:MIX_WIDTH] = (b_buf[c] * scale_b * gb_ref[...]).astype(BF16)
        yield

    def mix_pieces():
        attention = (mix_b(c) for c in range(n_chunks))
        gens = [next(attention), mix_a()]
        while gens:
            for gen in list(gens):
                try:
                    next(gen)
                    yield
                except StopIteration:
                    following = next(attention, None) if gen is gens[0] else None
                    if following is None:
                        gens.remove(gen)
                    else:
                        gens[0] = following

    def output_pieces():
        mix_cu[...] = mix_nx[...]
        res_slot = (step + 2) % X_RING_SLOTS
        for c0 in range(0, D_MODEL, MXU_COLS):
            def piece(c0=c0):
                o_ref[:, c0:c0 + MXU_COLS] = xring[res_slot, :, c0:c0 + MXU_COLS] + jnp.dot(
                    mix_cu[...], wout_ref[:, c0:c0 + MXU_COLS], preferred_element_type=F32)
            yield piece

    def run():
        out_gen = output_pieces()
        next(out_gen)()
        matmuls = list(out_gen)
        take_projection()
        matmuls += list(project_pieces())
        cast_ffn_weights()
        n_yields = n_chunks * MIX_B_PIECES + A_GROUPS * (n_chunks + 1) + n_chunks
        spread = len(matmuls) - TRAILING_MATMULS
        done = 0
        for k, _ in enumerate(mix_pieces(), start=1):
            want = (k * spread) // n_yields
            while done < want:
                matmuls[done]()
                done += 1
        for piece in matmuls[done:]:
            piece()

    @pl.when(step == 0)
    def _():
        row = lax.broadcasted_iota(jnp.int32, (CHUNK, CHUNK), 0)
        col = lax.broadcasted_iota(jnp.int32, (CHUNK, CHUNK), 1)
        for g in range(A_GROUPS):
            ws_bf[g] = jnp.where(row >= col, ws_ref[g], 0.0).astype(BF16)
            diag = jnp.where(row == col, bsp_ref[g:g + 1, :], 0.0)
            bs_ref[g] = jnp.broadcast_to(jnp.sum(diag, axis=1, keepdims=True), (CHUNK, CHUNK))
        for ref in (kvbuf, u_nx, v_nx, q_nx, kv_nx, mix_nx):
            ref[...] = jnp.zeros_like(ref)
        for slot in (2, 3):
            xring[slot] = jnp.zeros(xring.shape[1:], F32)
        x_copy(0).start()
        load_projection_weights()

    @pl.when(step + 1 < n_tiles)
    def _():
        x_copy(step + 1).start()

    @pl.when(step < n_tiles)
    def _():
        x_copy(step).wait()

    run()


def _mix_block(x2, g, w_in, lng, lnb, w_s, b_s, sinks, table, bucket, ga, gb, w_out, w_up,
               w_down, layer, tq, seq):
    t = x2.shape[0]
    n_tiles = t // tq
    n_chunks = tq // CHUNK
    proj = lambda w: pltpu.VMEM((tq, w), BF16)
    out_tile = lambda s: (jnp.clip(s - 2, 0, n_tiles - 1), 0)
    up_rows, dn_rows = D_MODEL // n_tiles, D_FF // n_tiles
    slab = lambda s: (jnp.minimum(s, n_tiles - 1), 0)
    return pl.pallas_call(
        functools.partial(_mix_block_kernel, layer, seq // CHUNK),
        out_shape=[jax.ShapeDtypeStruct((t, D_MODEL), F32),
                   jax.ShapeDtypeStruct((D_MODEL, D_FF), BF16),
                   jax.ShapeDtypeStruct((D_FF, D_MODEL), BF16)],
        grid=(n_tiles + 2,),
        in_specs=[
            pl.BlockSpec(memory_space=pl.ANY),
            _const_spec((1, D_MODEL)),
            pl.BlockSpec(memory_space=pl.ANY),
            _const_spec((1, A_WIDTH)),
            _const_spec((1, A_WIDTH)),
            _const_spec(w_s.shape),
            _const_spec(b_s.shape),
            pl.BlockSpec(memory_space=pltpu.SMEM),
            pl.BlockSpec(memory_space=pltpu.SMEM),
            _const_spec(bucket.shape),
            _const_spec((1, A_WIDTH)),
            _const_spec((1, B_WIDTH)),
            pl.BlockSpec(memory_space=pl.ANY),
            pl.BlockSpec((None, up_rows, D_FF), lambda s: (layer,) + slab(s)),
            pl.BlockSpec((None, dn_rows, D_MODEL), lambda s: (layer,) + slab(s)),
        ],
        out_specs=[pl.BlockSpec((tq, D_MODEL), out_tile),
                   pl.BlockSpec((up_rows, D_FF), slab),
                   pl.BlockSpec((dn_rows, D_MODEL), slab)],
        scratch_shapes=[
            pltpu.VMEM((A_GROUPS, CHUNK, CHUNK), BF16),
            pltpu.VMEM((A_GROUPS, CHUNK, CHUNK), F32),
            pltpu.VMEM((2, B_KV_HEADS, PAIRS_PER_KV, 2, CHUNK, 2 * CHUNK), F32),
            pltpu.VMEM((D_MODEL, w_in.shape[2]), BF16),
            pltpu.VMEM((MIX_WIDTH, D_MODEL), BF16),
            pltpu.VMEM((2, WEIGHT_STAGE_ROWS, w_in.shape[2]), F32),
            pltpu.VMEM((2, WEIGHT_STAGE_ROWS, D_MODEL), F32),
            pltpu.SemaphoreType.DMA((2, 2)),
            pltpu.VMEM((X_RING_SLOTS, tq, D_MODEL), F32),
            pltpu.SemaphoreType.DMA((X_RING_SLOTS,)),
            proj(D_MODEL),
            proj(A_WIDTH), proj(A_WIDTH), proj(B_WIDTH), proj(2 * KV_WIDTH),
            proj(A_WIDTH), proj(A_WIDTH), proj(B_WIDTH),
            pltpu.VMEM((tq + CHUNK, 2 * KV_WIDTH), BF16),
            proj(MIX_WIDTH), proj(MIX_WIDTH),
            pltpu.VMEM((n_chunks, CHUNK, A_WIDTH), F32),
            pltpu.VMEM((n_chunks, CHUNK, B_WIDTH), F32),
        ],
        compiler_params=_params(1),
        name="mix_block",
    )(x2, g, w_in, lng, lnb, w_s, b_s, sinks, table, bucket, ga, gb, w_out, w_up, w_down)


def _ffn_kernel(final_norm, h_hbm, g_ref, wu_ref, wd_ref, gf_ref, o_hbm,
                acc, nbuf, sem_in, sem_out):
    i = pl.program_id(0)
    f = pl.program_id(1)
    n_tiles = pl.num_programs(0)
    last_f = pl.num_programs(1) - 1
    tm = acc.shape[1]
    n_row_chunks = tm // FFN_NORM_ROWS
    slot = i % 2
    other = 1 - slot

    def h_copy(tile, s):
        return pltpu.make_async_copy(h_hbm.at[pl.ds(tile * tm, tm)], acc.at[s], sem_in)

    def o_copy(tile, s):
        return pltpu.make_async_copy(acc.at[s], o_hbm.at[pl.ds(tile * tm, tm)], sem_out.at[s])

    def chunk(k):
        return pl.ds(pl.multiple_of(k * FFN_NORM_ROWS, FFN_NORM_ROWS), FFN_NORM_ROWS)

    @pl.when(f == 0)
    def _():
        @pl.when(i == 0)
        def _():
            h_copy(0, 0).start()
        h_copy(i, slot).wait()

        def norm_rows(k, carry):
            hf = acc[slot, chunk(k), :]
            nbuf[chunk(k), :] = (hf * _rms_scale(hf) * g_ref[...]).astype(BF16)
            return carry
        lax.fori_loop(0, n_row_chunks, norm_rows, 0, unroll=True)

    @pl.when((f == 1) & (i + 1 < n_tiles))
    def _():
        @pl.when(i >= 1)
        def _():
            o_copy(i - 1, other).wait()
        h_copy(i + 1, other).start()

    z = jnp.maximum(jnp.dot(nbuf[...], wu_ref[...], preferred_element_type=F32), 0.0)
    update = jnp.dot((z * z).astype(BF16), wd_ref[...], preferred_element_type=F32)
    acc[slot] = acc[slot] + update

    @pl.when(f == last_f)
    def _():
        if final_norm:
            def final_rows(k, carry):
                y = acc[slot, chunk(k), :]
                acc[slot, chunk(k), :] = y * _rms_scale(y) * gf_ref[...]
                return carry
            lax.fori_loop(0, n_row_chunks, final_rows, 0, unroll=True)
        o_copy(i, slot).start()

        @pl.when(i == n_tiles - 1)
        def _():
            @pl.when(i >= 1)
            def _():
                o_copy(i - 1, other).wait()
            o_copy(i, slot).wait()


def _ffn(h, g, w_up, w_down, gf, final_norm, tm, tf):
    t = h.shape[0]
    assert D_FF // tf >= 2 and tm % FFN_NORM_ROWS == 0
    return pl.pallas_call(
        functools.partial(_ffn_kernel, final_norm),
        out_shape=jax.ShapeDtypeStruct((t, D_MODEL), F32),
        grid=(t // tm, D_FF // tf),
        in_specs=[
            pl.BlockSpec(memory_space=pl.ANY),
            _const_spec((1, D_MODEL)),
            pl.BlockSpec((D_MODEL, tf), lambda i, f: (0, f)),
            pl.BlockSpec((tf, D_MODEL), lambda i, f: (f, 0)),
            _const_spec((1, D_MODEL)),
        ],
        out_specs=pl.BlockSpec(memory_space=pl.ANY),
        scratch_shapes=[
            pltpu.VMEM((2, tm, D_MODEL), F32),
            pltpu.VMEM((tm, D_MODEL), BF16),
            pltpu.SemaphoreType.DMA(()),
            pltpu.SemaphoreType.DMA((2,)),
        ],
        compiler_params=_params(2),
        name="ffn",
    )(h, g, w_up, w_down, gf)


def _relative_buckets():
    i = jnp.arange(CHUNK)[:, None]
    j = jnp.arange(2 * CHUNK)[None, :]
    rel = i + CHUNK - j
    relc = jnp.maximum(rel, 0)
    n_exact = N_BUCKETS // 2
    relf = jnp.maximum(relc, n_exact).astype(F32)
    large = n_exact + (jnp.log(relf / n_exact) / math.log(MAX_DISTANCE / n_exact)
                       * (N_BUCKETS - n_exact)).astype(jnp.int32)
    large = jnp.minimum(large, N_BUCKETS - 1)
    bucket = jnp.where(relc < n_exact, relc, large)
    return jnp.where((rel >= 0) & (rel < CHUNK), bucket, -1).astype(jnp.int32)


def kernel(x, rel_bias_table, mix_norm_g, w_in, gate_norm_g, gate_norm_b, w_spatial, b_spatial,
           attn_sinks, out_norm_a_g, out_norm_b_g, w_out, ffn_norm_g, w_up, w_down, final_norm_g):
    bsz, seq, d = x.shape
    depth = w_in.shape[0]
    t = bsz * seq
    tq, tm = 2 * CHUNK, 1024
    assert d == D_MODEL and x.dtype == F32, (x.shape, x.dtype)
    assert seq % tq == 0 and t % tm == 0, (bsz, seq)
    assert w_in.shape[1:] == (D_MODEL, 2 * A_WIDTH + B_WIDTH + 2 * KV_WIDTH), w_in.shape
    assert w_out.shape[1:] == (MIX_WIDTH, D_MODEL) and w_up.shape[1:] == (D_MODEL, D_FF)
    assert w_down.shape[1:] == (D_FF, D_MODEL) and w_spatial.shape[1:] == (A_GROUPS, CHUNK, CHUNK)
    assert rel_bias_table.shape == (N_BUCKETS, B_HEADS) and attn_sinks.shape[1:] == (B_HEADS,)
    bucket = _relative_buckets()
    h = x.reshape(t, d)
    for layer in range(depth):
        h_mid, w_up_bf, w_down_bf = _mix_block(
            h, mix_norm_g[layer].reshape(1, d), w_in,
            gate_norm_g[layer].reshape(1, A_WIDTH), gate_norm_b[layer].reshape(1, A_WIDTH),
            w_spatial[layer], b_spatial[layer], attn_sinks[layer],
            rel_bias_table.astype(F32), bucket,
            out_norm_a_g[layer].reshape(1, A_WIDTH), out_norm_b_g[layer].reshape(1, B_WIDTH),
            w_out, w_up, w_down, layer=layer, tq=tq, seq=seq)
        h = _ffn(h_mid, ffn_norm_g[layer].reshape(1, d), w_up_bf, w_down_bf,
                 final_norm_g.reshape(1, d), final_norm=layer == depth - 1, tm=tm, tf=2048)
    return h.reshape(bsz, seq, d)
```

```python
import functools
import math

import jax
import jax.numpy as jnp
from jax import lax
from jax.experimental import pallas as pl
from jax.experimental.pallas import tpu as pltpu

D_MODEL = 2048
CHUNK = 128
A_GROUPS = 8
A_GROUP_DIM = 128
A_WIDTH = A_GROUPS * A_GROUP_DIM
HEAD_DIM = 64
B_HEADS = 16
B_KV_HEADS = 2
Q_PER_KV = B_HEADS // B_KV_HEADS
PAIRS_PER_KV = Q_PER_KV // 2
B_WIDTH = B_HEADS * HEAD_DIM
KV_WIDTH = B_KV_HEADS * HEAD_DIM
N_BUCKETS = 32
MAX_DISTANCE = 128
MIX_WIDTH = A_WIDTH + B_WIDTH
D_FF = 4 * D_MODEL
EPS = 1e-5
NEG = -1e30
LOG2E = 1.4426950408889634
GELU_TANH_SCALE = math.sqrt(2.0 / math.pi)
GELU_CUBIC = 0.044715

V7X_VMEM_LIMIT_BYTES = 62 * 1024 * 1024
MXU_COLS = 256
FFN_NORM_ROWS = 128
TRAILING_MATMULS = 2
WEIGHT_STAGE_ROWS = 128
RS_LANES = 128
MIX_B_PIECES = B_KV_HEADS * (PAIRS_PER_KV + 2) + 1

BF16 = jnp.bfloat16
F32 = jnp.float32


def _rms_scale(xf):
    return lax.rsqrt(jnp.mean(xf * xf, axis=-1, keepdims=True) + EPS)


def _gelu(x):
    a = -2.0 * GELU_TANH_SCALE * LOG2E
    return x / (1.0 + jnp.exp2(x * (x * x * (a * GELU_CUBIC) + a)))


def _params(n_axes):
    return pltpu.CompilerParams(
        dimension_semantics=("arbitrary",) * n_axes,
        vmem_limit_bytes=V7X_VMEM_LIMIT_BYTES)


def _const_spec(shape):
    return pl.BlockSpec(shape, lambda *_: (0,) * len(shape), pipeline_mode=pl.Buffered(1))


def _half_swapped_pair(t):
    lane = lax.broadcasted_iota(jnp.int32, t.shape, 1)
    lo = lane < HEAD_DIM
    tr = pltpu.roll(t, HEAD_DIM, axis=1)
    zero = jnp.zeros_like(t)
    kv0 = jnp.concatenate([jnp.where(lo, t, zero), jnp.where(lo, zero, tr)], axis=0)
    kv1 = jnp.concatenate([jnp.where(lo, tr, zero), jnp.where(lo, zero, t)], axis=0)
    return kv0, kv1


def _mix_block_kernel(layer, chunks_per_seq,
                      x1_ref, x3_ref, g_ref, win_hbm, lng_ref, lnb_ref, ws_ref, bsp_ref,
                      sinks_ref, table_ref, bucket_ref, ga_ref, gb_ref, wout_hbm, wup_ref, wdn_ref,
                      gff_ref,
                      o_ref, rs_ref, wup_bf_ref, wdn_bf_ref,
                      ws_bf, bs_ref, gcol, bias_ref, win_ref, wout_ref, stage_in, stage_out, stage_sem,
                      n_buf, u_nx, v_nx, q_nx, kv_nx, u_cu, v_cu, q_cu, kvbuf,
                      mix_nx, mix_cu, a_buf, b_buf):
    tq = x1_ref.shape[0]
    n_chunks = tq // CHUNK
    step = pl.program_id(0)

    def head_bias(h):
        bucket = bucket_ref[...]
        col = lax.broadcasted_iota(jnp.int32, bucket.shape, 1)
        val = jnp.full(bucket.shape, NEG, F32)
        for b in range(N_BUCKETS):
            val = jnp.where(bucket == b, table_ref[b, h] * LOG2E, val)
        g, p, par = h // Q_PER_KV, (h % Q_PER_KV) // 2, h % 2
        bias_ref[0, g, p, par] = val
        bias_ref[1, g, p, par] = jnp.where(col < CHUNK, NEG, val)

    def load_projection_weights():
        rows = stage_in.shape[1]
        n_slabs = D_MODEL // rows
        assert n_slabs == B_HEADS

        def copies(k, slot):
            src = pl.ds(k * rows, rows)
            return (pltpu.make_async_copy(win_hbm.at[layer, src], stage_in.at[slot],
                                          stage_sem.at[0, slot]),
                    pltpu.make_async_copy(wout_hbm.at[layer, src], stage_out.at[slot],
                                          stage_sem.at[1, slot]))

        for cp in copies(0, 0):
            cp.start()

        def slab(k, carry):
            slot = k % 2

            @pl.when(k + 1 < n_slabs)
            def _():
                for cp in copies(k + 1, 1 - slot):
                    cp.start()
            head_bias(k)
            for cp in copies(k, slot):
                cp.wait()
            dst = pl.ds(pl.multiple_of(k * rows, rows), rows)
            win_ref[dst, :] = stage_in[slot].astype(BF16)
            wout_ref[dst, :] = stage_out[slot].astype(BF16)
            return carry
        lax.fori_loop(0, n_slabs, slab, 0)

    def cast_ffn_weights():
        rows = wup_ref.shape[0]
        slab = jnp.minimum(step, pl.num_programs(0) - 3)
        gain = gcol[pl.ds(pl.multiple_of(slab * rows, rows), rows), :]
        wup_bf_ref[...] = (wup_ref[...] * gain).astype(BF16)
        wdn_bf_ref[...] = wdn_ref[...].astype(BF16)

    def project_pieces():
        xf = x1_ref[...]
        n_buf[...] = (xf * _rms_scale(xf) * g_ref[...]).astype(BF16)
        col = 0
        for dst in (u_nx, v_nx, q_nx, kv_nx):
            for c0 in range(0, dst.shape[1], MXU_COLS):
                def piece(dst=dst, c0=c0, col=col):
                    dst[:, c0:c0 + MXU_COLS] = jnp.dot(
                        n_buf[...], win_ref[:, col + c0:col + c0 + MXU_COLS],
                        preferred_element_type=F32).astype(BF16)
                yield piece
            col += dst.shape[1]

    def take_projection():
        kvbuf[0:CHUNK, :] = kvbuf[tq:tq + CHUNK, :]
        kvbuf[CHUNK:CHUNK + tq, :] = kv_nx[...]
        u_cu[...] = u_nx[...]
        v_cu[...] = v_nx[...]
        q_cu[...] = q_nx[...]

    def mix_a():
        ss_a = [jnp.zeros((CHUNK, A_GROUP_DIM), F32) for _ in range(n_chunks)]
        for g in range(A_GROUPS):
            cols = slice(g * A_GROUP_DIM, (g + 1) * A_GROUP_DIM)
            ugs, vns = [], []
            for c in range(n_chunks):
                rows = slice(c * CHUNK, (c + 1) * CHUNK)
                ugs.append(_gelu(u_cu[rows, cols].astype(F32)))
                vg = _gelu(v_cu[rows, cols].astype(F32))
                mu = jnp.mean(vg, axis=-1, keepdims=True)
                vc = vg - mu
                var = jnp.mean(vc * vc, axis=-1, keepdims=True)
                vn = vc * lax.rsqrt(var + EPS) * lng_ref[:, cols] + lnb_ref[:, cols]
                vns.append(vn.astype(BF16))
                yield
            mixed = jnp.dot(ws_bf[g], jnp.concatenate(vns, axis=1), preferred_element_type=F32)
            for c in range(n_chunks):
                ag = ugs[c] * (mixed[:, c * A_GROUP_DIM:(c + 1) * A_GROUP_DIM] + bs_ref[g])
                a_buf[c, :, cols] = ag
                ss_a[c] = ss_a[c] + ag * ag
            yield
        for c in range(n_chunks):
            rows = slice(c * CHUNK, (c + 1) * CHUNK)
            scale_a = lax.rsqrt(jnp.sum(ss_a[c], axis=-1, keepdims=True) * (1.0 / A_WIDTH) + EPS)
            mix_nx[rows, 0:A_WIDTH] = (a_buf[c] * scale_a * ga_ref[...]).astype(BF16)
            yield

    def mix_b(c):
        rows = slice(c * CHUNK, (c + 1) * CHUNK)
        band = kvbuf[c * CHUNK:(c + 2) * CHUNK, :]
        k_stacks = _half_swapped_pair(band[:, :KV_WIDTH].astype(F32) * (HEAD_DIM ** -0.5 * LOG2E))
        v_stacks = _half_swapped_pair(band[:, KV_WIDTH:].astype(F32))
        chunk_id = (step - 1) * n_chunks + c
        first = (chunk_id % chunks_per_seq == 0).astype(jnp.int32)
        lane = lax.broadcasted_iota(jnp.int32, (CHUNK, 2 * HEAD_DIM), 1)
        lo = lane < HEAD_DIM
        srow = lax.broadcasted_iota(jnp.int32, (4 * CHUNK, 2 * HEAD_DIM), 0)
        slane = lax.broadcasted_iota(jnp.int32, (4 * CHUNK, 2 * HEAD_DIM), 1)
        row_sum_cols = ((srow < 2 * CHUNK) == (slane < HEAD_DIM)).astype(BF16)
        ss_b = jnp.zeros((CHUNK, 2 * HEAD_DIM), F32)
        for g in range(B_KV_HEADS):
            q_stack = jnp.concatenate(
                [q_cu[rows, (PAIRS_PER_KV * g + p) * 128:(PAIRS_PER_KV * g + p + 1) * 128]
                 for p in range(PAIRS_PER_KV)], axis=0)
            s_all = lax.dot_general(q_stack, k_stacks[g].astype(BF16),
                                    (((1,), (1,)), ((), ())),
                                    preferred_element_type=F32)
            yield
            p_rows = []
            sink_terms = []
            for p in range(PAIRS_PER_KV):
                p_cols = []
                sink_pair = []
                for par in range(2):
                    head = Q_PER_KV * g + 2 * p + par
                    rs = slice(p * CHUNK, (p + 1) * CHUNK)
                    cs = slice(par * 2 * CHUNK, (par + 1) * 2 * CHUNK)
                    s = s_all[rs, cs] + bias_ref[first, g, p, par]
                    sink = sinks_ref[head] * LOG2E
                    m = jnp.maximum(jnp.max(s, axis=-1, keepdims=True), sink)
                    p_cols.append(jnp.exp2(s - m).astype(BF16))
                    sink_pair.append(jnp.exp2(sink - m))
                p_rows.append(jnp.concatenate(p_cols, axis=1))
                sink_terms.append(sink_pair)
                yield
            probs = jnp.concatenate(p_rows, axis=0)
            values = jnp.concatenate([v_stacks[g].astype(BF16), row_sum_cols], axis=1)
            o_all = jnp.dot(probs, values, preferred_element_type=F32)
            for p in range(PAIRS_PER_KV):
                blk = PAIRS_PER_KV * g + p
                o_p = o_all[p * CHUNK:(p + 1) * CHUNK]
                den = o_p[:, 2 * HEAD_DIM:] + jnp.where(lo, sink_terms[p][0], sink_terms[p][1])
                o = o_p[:, :2 * HEAD_DIM] / den
                b_buf[c, :, blk * 128:(blk + 1) * 128] = o
                ss_b = ss_b + o * o
            yield
        scale_b = lax.rsqrt(jnp.sum(ss_b, axis=-1, keepdims=True) * (1.0 / B_WIDTH) + EPS)
        mix_nx[rows, A_WIDTH:MIX_WIDTH] = (b_buf[c] * scale_b * gb_ref[...]).astype(BF16)
        yield

    def mix_pieces():
        attention = (mix_b(c) for c in range(n_chunks))
        gens = [next(attention), mix_a()]
        while gens:
            for gen in list(gens):
                try:
                    next(gen)
                    yield
                except StopIteration:
                    following = next(attention, None) if gen is gens[0] else None
                    if following is None:
                        gens.remove(gen)
                    else:
                        gens[0] = following

    def output_pieces():
        mix_cu[...] = mix_nx[...]
        sumsq = []
        for c0 in range(0, D_MODEL, MXU_COLS):
            def piece(c0=c0):
                hp = x3_ref[:, c0:c0 + MXU_COLS] + jnp.dot(
                    mix_cu[...], wout_ref[:, c0:c0 + MXU_COLS], preferred_element_type=F32)
                o_ref[:, c0:c0 + MXU_COLS] = hp
                sq = hp * hp
                sumsq.append(sq[:, :MXU_COLS // 2] + sq[:, MXU_COLS // 2:])
            yield piece

        def scale_piece():
            total = functools.reduce(lambda a, b: a + b, sumsq)
            scale = lax.rsqrt(jnp.sum(total, axis=-1, keepdims=True) * (1.0 / D_MODEL) + EPS)
            rs_ref[...] = jnp.broadcast_to(scale, rs_ref.shape)
        yield scale_piece

    def run():
        out_gen = output_pieces()
        next(out_gen)()
        matmuls = list(out_gen)
        take_projection()
        matmuls += list(project_pieces())
        cast_ffn_weights()
        n_yields = n_chunks * MIX_B_PIECES + A_GROUPS * (n_chunks + 1) + n_chunks
        spread = len(matmuls) - TRAILING_MATMULS
        done = 0
        for k, _ in enumerate(mix_pieces(), start=1):
            want = (k * spread) // n_yields
            while done < want:
                matmuls[done]()
                done += 1
        for piece in matmuls[done:]:
            piece()

    @pl.when(step == 0)
    def _():
        row = lax.broadcasted_iota(jnp.int32, (CHUNK, CHUNK), 0)
        col = lax.broadcasted_iota(jnp.int32, (CHUNK, CHUNK), 1)
        for g in range(A_GROUPS):
            ws_bf[g] = jnp.where(row >= col, ws_ref[g], 0.0).astype(BF16)
            diag = jnp.where(row == col, bsp_ref[g:g + 1, :], 0.0)
            bs_ref[g] = jnp.broadcast_to(jnp.sum(diag, axis=1, keepdims=True), (CHUNK, CHUNK))
        for c0 in range(0, D_MODEL, CHUNK):
            diag = jnp.where(row == col, gff_ref[:, c0:c0 + CHUNK], 0.0)
            gcol[c0:c0 + CHUNK, :] = jnp.sum(diag, axis=1, keepdims=True)
        for ref in (kvbuf, u_nx, v_nx, q_nx, kv_nx, mix_nx):
            ref[...] = jnp.zeros_like(ref)
        load_projection_weights()

    run()


def _mix_block(x2, g, w_in, lng, lnb, w_s, b_s, sinks, table, bucket, ga, gb, w_out, w_up,
               w_down, g_ffn, layer, tq, seq):
    t = x2.shape[0]
    n_tiles = t // tq
    n_chunks = tq // CHUNK
    proj = lambda w: pltpu.VMEM((tq, w), BF16)
    out_tile = lambda s: (jnp.clip(s - 2, 0, n_tiles - 1), 0)
    up_rows, dn_rows = D_MODEL // n_tiles, D_FF // n_tiles
    slab = lambda s: (jnp.minimum(s, n_tiles - 1), 0)
    return pl.pallas_call(
        functools.partial(_mix_block_kernel, layer, seq // CHUNK),
        out_shape=[jax.ShapeDtypeStruct((t, D_MODEL), F32),
                   jax.ShapeDtypeStruct((t, RS_LANES), F32),
                   jax.ShapeDtypeStruct((D_MODEL, D_FF), BF16),
                   jax.ShapeDtypeStruct((D_FF, D_MODEL), BF16)],
        grid=(n_tiles + 2,),
        in_specs=[
            pl.BlockSpec((tq, D_MODEL), lambda s: (jnp.minimum(s, n_tiles - 1), 0)),
            pl.BlockSpec((tq, D_MODEL), out_tile),
            _const_spec((1, D_MODEL)),
            pl.BlockSpec(memory_space=pl.ANY),
            _const_spec((1, A_WIDTH)),
            _const_spec((1, A_WIDTH)),
            _const_spec(w_s.shape),
            _const_spec(b_s.shape),
            pl.BlockSpec(memory_space=pltpu.SMEM),
            pl.BlockSpec(memory_space=pltpu.SMEM),
            _const_spec(bucket.shape),
            _const_spec((1, A_WIDTH)),
            _const_spec((1, B_WIDTH)),
            pl.BlockSpec(memory_space=pl.ANY),
            pl.BlockSpec((None, up_rows, D_FF), lambda s: (layer,) + slab(s)),
            pl.BlockSpec((None, dn_rows, D_MODEL), lambda s: (layer,) + slab(s)),
            _const_spec((1, D_MODEL)),
        ],
        out_specs=[pl.BlockSpec((tq, D_MODEL), out_tile),
                   pl.BlockSpec((tq, RS_LANES), out_tile),
                   pl.BlockSpec((up_rows, D_FF), slab),
                   pl.BlockSpec((dn_rows, D_MODEL), slab)],
        scratch_shapes=[
            pltpu.VMEM((A_GROUPS, CHUNK, CHUNK), BF16),
            pltpu.VMEM((A_GROUPS, CHUNK, CHUNK), F32),
            pltpu.VMEM((D_MODEL, 1), F32),
            pltpu.VMEM((2, B_KV_HEADS, PAIRS_PER_KV, 2, CHUNK, 2 * CHUNK), F32),
            pltpu.VMEM((D_MODEL, w_in.shape[2]), BF16),
            pltpu.VMEM((MIX_WIDTH, D_MODEL), BF16),
            pltpu.VMEM((2, WEIGHT_STAGE_ROWS, w_in.shape[2]), F32),
            pltpu.VMEM((2, WEIGHT_STAGE_ROWS, D_MODEL), F32),
            pltpu.SemaphoreType.DMA((2, 2)),
            proj(D_MODEL),
            proj(A_WIDTH), proj(A_WIDTH), proj(B_WIDTH), proj(2 * KV_WIDTH),
            proj(A_WIDTH), proj(A_WIDTH), proj(B_WIDTH),
            pltpu.VMEM((tq + CHUNK, 2 * KV_WIDTH), BF16),
            proj(MIX_WIDTH), proj(MIX_WIDTH),
            pltpu.VMEM((n_chunks, CHUNK, A_WIDTH), F32),
            pltpu.VMEM((n_chunks, CHUNK, B_WIDTH), F32),
        ],
        compiler_params=_params(1),
        name="mix_block",
    )(x2, x2, g, w_in, lng, lnb, w_s, b_s, sinks, table, bucket, ga, gb, w_out, w_up, w_down,
      g_ffn)


def _ffn_kernel(final_norm, h_hbm, rs_hbm, wu_ref, wd_ref, gf_ref, o_hbm,
                acc, nbuf, rs_stage, sem_in, sem_out):
    i = pl.program_id(0)
    f = pl.program_id(1)
    n_tiles = pl.num_programs(0)
    last_f = pl.num_programs(1) - 1
    tm = acc.shape[1]
    n_row_chunks = tm // FFN_NORM_ROWS
    slot = i % 2
    other = 1 - slot

    def h_copies(tile, s):
        rows = pl.ds(tile * tm, tm)
        return (pltpu.make_async_copy(h_hbm.at[rows], acc.at[s], sem_in.at[0]),
                pltpu.make_async_copy(rs_hbm.at[rows], rs_stage, sem_in.at[1]))

    def o_copy(tile, s):
        return pltpu.make_async_copy(acc.at[s], o_hbm.at[pl.ds(tile * tm, tm)], sem_out.at[s])

    def chunk(k):
        return pl.ds(pl.multiple_of(k * FFN_NORM_ROWS, FFN_NORM_ROWS), FFN_NORM_ROWS)

    @pl.when(f == 0)
    def _():
        @pl.when(i == 0)
        def _():
            for cp in h_copies(0, 0):
                cp.start()
        for cp in h_copies(i, slot):
            cp.wait()

        def norm_rows(k, carry):
            scale = rs_stage[chunk(k), 0:1]
            nbuf[chunk(k), :] = (acc[slot, chunk(k), :] * scale).astype(BF16)
            return carry
        lax.fori_loop(0, n_row_chunks, norm_rows, 0, unroll=True)

    @pl.when((f == 1) & (i + 1 < n_tiles))
    def _():
        @pl.when(i >= 1)
        def _():
            o_copy(i - 1, other).wait()
        for cp in h_copies(i + 1, other):
            cp.start()

    z = jnp.maximum(jnp.dot(nbuf[...], wu_ref[...], preferred_element_type=F32), 0.0)
    update = jnp.dot((z * z).astype(BF16), wd_ref[...], preferred_element_type=F32)
    acc[slot] = acc[slot] + update

    @pl.when(f == last_f)
    def _():
        if final_norm:
            def final_rows(k, carry):
                y = acc[slot, chunk(k), :]
                acc[slot, chunk(k), :] = y * _rms_scale(y) * gf_ref[...]
                return carry
            lax.fori_loop(0, n_row_chunks, final_rows, 0, unroll=True)
        o_copy(i, slot).start()

        @pl.when(i == n_tiles - 1)
        def _():
            @pl.when(i >= 1)
            def _():
                o_copy(i - 1, other).wait()
            o_copy(i, slot).wait()


def _ffn(h, rs, w_up, w_down, gf, final_norm, tm, tf):
    t = h.shape[0]
    assert D_FF // tf >= 2 and tm % FFN_NORM_ROWS == 0
    return pl.pallas_call(
        functools.partial(_ffn_kernel, final_norm),
        out_shape=jax.ShapeDtypeStruct((t, D_MODEL), F32),
        grid=(t // tm, D_FF // tf),
        in_specs=[
            pl.BlockSpec(memory_space=pl.ANY),
            pl.BlockSpec(memory_space=pl.ANY),
            pl.BlockSpec((D_MODEL, tf), lambda i, f: (0, f)),
            pl.BlockSpec((tf, D_MODEL), lambda i, f: (f, 0)),
            _const_spec((1, D_MODEL)),
        ],
        out_specs=pl.BlockSpec(memory_space=pl.ANY),
        scratch_shapes=[
            pltpu.VMEM((2, tm, D_MODEL), F32),
            pltpu.VMEM((tm, D_MODEL), BF16),
            pltpu.VMEM((tm, RS_LANES), F32),
            pltpu.SemaphoreType.DMA((2,)),
            pltpu.SemaphoreType.DMA((2,)),
        ],
        compiler_params=_params(2),
        name="ffn",
    )(h, rs, w_up, w_down, gf)


def _relative_buckets():
    i = jnp.arange(CHUNK)[:, None]
    j = jnp.arange(2 * CHUNK)[None, :]
    rel = i + CHUNK - j
    relc = jnp.maximum(rel, 0)
    n_exact = N_BUCKETS // 2
    relf = jnp.maximum(relc, n_exact).astype(F32)
    large = n_exact + (jnp.log(relf / n_exact) / math.log(MAX_DISTANCE / n_exact)
                       * (N_BUCKETS - n_exact)).astype(jnp.int32)
    large = jnp.minimum(large, N_BUCKETS - 1)
    bucket = jnp.where(relc < n_exact, relc, large)
    return jnp.where((rel >= 0) & (rel < CHUNK), bucket, -1).astype(jnp.int32)


def kernel(x, rel_bias_table, mix_norm_g, w_in, gate_norm_g, gate_norm_b, w_spatial, b_spatial,
           attn_sinks, out_norm_a_g, out_norm_b_g, w_out, ffn_norm_g, w_up, w_down, final_norm_g):
    bsz, seq, d = x.shape
    depth = w_in.shape[0]
    t = bsz * seq
    tq, tm = 2 * CHUNK, 1024
    assert d == D_MODEL and x.dtype == F32, (x.shape, x.dtype)
    assert seq % tq == 0 and t % tm == 0, (bsz, seq)
    assert w_in.shape[1:] == (D_MODEL, 2 * A_WIDTH + B_WIDTH + 2 * KV_WIDTH), w_in.shape
    assert w_out.shape[1:] == (MIX_WIDTH, D_MODEL) and w_up.shape[1:] == (D_MODEL, D_FF)
    assert w_down.shape[1:] == (D_FF, D_MODEL) and w_spatial.shape[1:] == (A_GROUPS, CHUNK, CHUNK)
    assert rel_bias_table.shape == (N_BUCKETS, B_HEADS) and attn_sinks.shape[1:] == (B_HEADS,)
    bucket = _relative_buckets()
    h = x.reshape(t, d)
    for layer in range(depth):
        h_mid, h_mid_rs, w_up_bf, w_down_bf = _mix_block(
            h, mix_norm_g[layer].reshape(1, d), w_in,
            gate_norm_g[layer].reshape(1, A_WIDTH), gate_norm_b[layer].reshape(1, A_WIDTH),
            w_spatial[layer], b_spatial[layer], attn_sinks[layer],
            rel_bias_table.astype(F32), bucket,
            out_norm_a_g[layer].reshape(1, A_WIDTH), out_norm_b_g[layer].reshape(1, B_WIDTH),
            w_out, w_up, w_down, ffn_norm_g[layer].reshape(1, d), layer=layer, tq=tq, seq=seq)
        h = _ffn(h_mid, h_mid_rs, w_up_bf, w_down_bf,
                 final_norm_g.reshape(1, d), final_norm=layer == depth - 1, tm=tm, tf=2048)
    return h.reshape(bsz, seq, d)
```

```python
import functools
import math

import jax
import jax.numpy as jnp
from jax import lax
from jax.experimental import pallas as pl
from jax.experimental.pallas import tpu as pltpu

D_MODEL = 2048
CHUNK = 128
A_GROUPS = 8
A_GROUP_DIM = 128
A_WIDTH = A_GROUPS * A_GROUP_DIM
HEAD_DIM = 64
B_HEADS = 16
B_KV_HEADS = 2
Q_PER_KV = B_HEADS // B_KV_HEADS
PAIRS_PER_KV = Q_PER_KV // 2
B_WIDTH = B_HEADS * HEAD_DIM
KV_WIDTH = B_KV_HEADS * HEAD_DIM
N_BUCKETS = 32
MAX_DISTANCE = 128
MIX_WIDTH = A_WIDTH + B_WIDTH
D_FF = 4 * D_MODEL
EPS = 1e-5
NEG = -1e30
LOG2E = 1.4426950408889634
GELU_TANH_SCALE = math.sqrt(2.0 / math.pi)
GELU_CUBIC = 0.044715

V7X_VMEM_LIMIT_BYTES = 62 * 1024 * 1024
MXU_COLS = 256
FFN_NORM_ROWS = 128
TRAILING_MATMULS = 2
WEIGHT_STAGE_ROWS = 128
RS_LANES = 128
MIX_B_PIECES = B_KV_HEADS * (PAIRS_PER_KV + 2) + 1

BF16 = jnp.bfloat16
F32 = jnp.float32


def _rms_scale(xf):
    return lax.rsqrt(jnp.mean(xf * xf, axis=-1, keepdims=True) + EPS)


def _gelu(x):
    a = -2.0 * GELU_TANH_SCALE * LOG2E
    return x / (1.0 + jnp.exp2(x * (x * x * (a * GELU_CUBIC) + a)))


def _params(n_axes):
    return pltpu.CompilerParams(
        dimension_semantics=("arbitrary",) * n_axes,
        vmem_limit_bytes=V7X_VMEM_LIMIT_BYTES)


def _const_spec(shape):
    return pl.BlockSpec(shape, lambda *_: (0,) * len(shape), pipeline_mode=pl.Buffered(1))


def _half_swapped_pair(t):
    lane = lax.broadcasted_iota(jnp.int32, t.shape, 1)
    lo = lane < HEAD_DIM
    tr = pltpu.roll(t, HEAD_DIM, axis=1)
    zero = jnp.zeros_like(t)
    kv0 = jnp.concatenate([jnp.where(lo, t, zero), jnp.where(lo, zero, tr)], axis=0)
    kv1 = jnp.concatenate([jnp.where(lo, tr, zero), jnp.where(lo, zero, t)], axis=0)
    return kv0, kv1


def _mix_block_kernel(layer, chunks_per_seq,
                      x1_ref, x3_ref, g_ref, win_hbm, lng_ref, lnb_ref, ws_ref, bsp_ref,
                      sinks_ref, table_ref, bucket_ref, ga_ref, gb_ref, wout_hbm, wup_ref, wdn_ref,
                      gff_ref,
                      o_ref, rs_ref, wup_bf_ref, wdn_bf_ref,
                      ws_bf, bs_ref, gcol, bias_ref, win_ref, wout_ref, stage_in, stage_out, stage_sem,
                      n_buf, u_nx, v_nx, q_nx, kv_nx, u_cu, v_cu, q_cu, kvbuf,
                      mix_nx, mix_cu, a_buf, b_buf):
    tq = x1_ref.shape[0]
    n_chunks = tq // CHUNK
    step = pl.program_id(0)

    def head_bias(h):
        bucket = bucket_ref[...]
        col = lax.broadcasted_iota(jnp.int32, bucket.shape, 1)
        val = jnp.full(bucket.shape, NEG, F32)
        for b in range(N_BUCKETS):
            val = jnp.where(bucket == b, table_ref[b, h] * LOG2E, val)
        g, p, par = h // Q_PER_KV, (h % Q_PER_KV) // 2, h % 2
        bias_ref[0, g, p, par] = val
        bias_ref[1, g, p, par] = jnp.where(col < CHUNK, NEG, val)

    def load_projection_weights():
        rows = stage_in.shape[1]
        n_slabs = D_MODEL // rows
        assert n_slabs == B_HEADS

        def copies(k, slot):
            src = pl.ds(k * rows, rows)
            return (pltpu.make_async_copy(win_hbm.at[layer, src], stage_in.at[slot],
                                          stage_sem.at[0, slot]),
                    pltpu.make_async_copy(wout_hbm.at[layer, src], stage_out.at[slot],
                                          stage_sem.at[1, slot]))

        for cp in copies(0, 0):
            cp.start()

        def slab(k, carry):
            slot = k % 2

            @pl.when(k + 1 < n_slabs)
            def _():
                for cp in copies(k + 1, 1 - slot):
                    cp.start()
            head_bias(k)
            for cp in copies(k, slot):
                cp.wait()
            dst = pl.ds(pl.multiple_of(k * rows, rows), rows)
            win_ref[dst, :] = stage_in[slot].astype(BF16)
            wout_ref[dst, :] = stage_out[slot].astype(BF16)
            return carry
        lax.fori_loop(0, n_slabs, slab, 0)

    def cast_ffn_weights():
        rows = wup_ref.shape[0]
        slab = jnp.minimum(step, pl.num_programs(0) - 3)
        gain = gcol[pl.ds(pl.multiple_of(slab * rows, rows), rows), :]
        wup_bf_ref[...] = (wup_ref[...] * gain).astype(BF16)
        wdn_bf_ref[...] = wdn_ref[...].astype(BF16)

    def project_pieces():
        xf = x1_ref[...]
        n_buf[...] = (xf * _rms_scale(xf) * g_ref[...]).astype(BF16)
        col = 0
        for dst in (u_nx, v_nx, q_nx, kv_nx):
            for c0 in range(0, dst.shape[1], MXU_COLS):
                def piece(dst=dst, c0=c0, col=col):
                    dst[:, c0:c0 + MXU_COLS] = jnp.dot(
                        n_buf[...], win_ref[:, col + c0:col + c0 + MXU_COLS],
                        preferred_element_type=F32).astype(BF16)
                yield piece
            col += dst.shape[1]

    def take_projection():
        kvbuf[0:CHUNK, :] = kvbuf[tq:tq + CHUNK, :]
        kvbuf[CHUNK:CHUNK + tq, :] = kv_nx[...]
        u_cu[...] = u_nx[...]
        v_cu[...] = v_nx[...]
        q_cu[...] = q_nx[...]

    def mix_a():
        ss_a = [jnp.zeros((CHUNK, A_GROUP_DIM), F32) for _ in range(n_chunks)]
        for g in range(A_GROUPS):
            cols = slice(g * A_GROUP_DIM, (g + 1) * A_GROUP_DIM)
            ugs, vns = [], []
            for c in range(n_chunks):
                rows = slice(c * CHUNK, (c + 1) * CHUNK)
                ugs.append(_gelu(u_cu[rows, cols].astype(F32)))
                vg = _gelu(v_cu[rows, cols].astype(F32))
                mu = jnp.mean(vg, axis=-1, keepdims=True)
                vc = vg - mu
                var = jnp.mean(vc * vc, axis=-1, keepdims=True)
                vn = vc * lax.rsqrt(var + EPS) * lng_ref[:, cols] + lnb_ref[:, cols]
                vns.append(vn.astype(BF16))
                yield
            mixed = jnp.dot(ws_bf[g], jnp.concatenate(vns, axis=1), preferred_element_type=F32)
            for c in range(n_chunks):
                ag = ugs[c] * (mixed[:, c * A_GROUP_DIM:(c + 1) * A_GROUP_DIM] + bs_ref[g])
                a_buf[c, :, cols] = ag
                ss_a[c] = ss_a[c] + ag * ag
            yield
        for c in range(n_chunks):
            rows = slice(c * CHUNK, (c + 1) * CHUNK)
            scale_a = lax.rsqrt(jnp.sum(ss_a[c], axis=-1, keepdims=True) * (1.0 / A_WIDTH) + EPS)
            mix_nx[rows, 0:A_WIDTH] = (a_buf[c] * scale_a * ga_ref[...]).astype(BF16)
            yield

    def mix_b(c):
        rows = slice(c * CHUNK, (c + 1) * CHUNK)
        band = kvbuf[c * CHUNK:(c + 2) * CHUNK, :]
        k_stacks = _half_swapped_pair(band[:, :KV_WIDTH].astype(F32) * (HEAD_DIM ** -0.5 * LOG2E))
        v_stacks = _half_swapped_pair(band[:, KV_WIDTH:].astype(F32))
        chunk_id = (step - 1) * n_chunks + c
        first = (chunk_id % chunks_per_seq == 0).astype(jnp.int32)
        lane = lax.broadcasted_iota(jnp.int32, (CHUNK, 2 * HEAD_DIM), 1)
        lo = lane < HEAD_DIM
        srow = lax.broadcasted_iota(jnp.int32, (4 * CHUNK, 2 * HEAD_DIM), 0)
        slane = lax.broadcasted_iota(jnp.int32, (4 * CHUNK, 2 * HEAD_DIM), 1)
        row_sum_cols = ((srow < 2 * CHUNK) == (slane < HEAD_DIM)).astype(BF16)
        ss_b = jnp.zeros((CHUNK, 2 * HEAD_DIM), F32)
        for g in range(B_KV_HEADS):
            q_stack = jnp.concatenate(
                [q_cu[rows, (PAIRS_PER_KV * g + p) * 128:(PAIRS_PER_KV * g + p + 1) * 128]
                 for p in range(PAIRS_PER_KV)], axis=0)
            s_all = lax.dot_general(q_stack, k_stacks[g].astype(BF16),
                                    (((1,), (1,)), ((), ())),
                                    preferred_element_type=F32)
            yield
            p_rows = []
            sink_terms = []
            for p in range(PAIRS_PER_KV):
                p_cols = []
                sink_pair = []
                for par in range(2):
                    head = Q_PER_KV * g + 2 * p + par
                    rs = slice(p * CHUNK, (p + 1) * CHUNK)
                    cs = slice(par * 2 * CHUNK, (par + 1) * 2 * CHUNK)
                    s = s_all[rs, cs] + bias_ref[first, g, p, par]
                    sink = sinks_ref[head] * LOG2E
                    m = jnp.maximum(jnp.max(s, axis=-1, keepdims=True), sink)
                    p_cols.append(jnp.exp2(s - m).astype(BF16))
                    sink_pair.append(jnp.exp2(sink - m))
                p_rows.append(jnp.concatenate(p_cols, axis=1))
                sink_terms.append(sink_pair)
                yield
            probs = jnp.concatenate(p_rows, axis=0)
            values = jnp.concatenate([v_stacks[g].astype(BF16), row_sum_cols], axis=1)
            o_all = jnp.dot(probs, values, preferred_element_type=F32)
            for p in range(PAIRS_PER_KV):
                blk = PAIRS_PER_KV * g + p
                o_p = o_all[p * CHUNK:(p + 1) * CHUNK]
                den = o_p[:, 2 * HEAD_DIM:] + jnp.where(lo, sink_terms[p][0], sink_terms[p][1])
                o = o_p[:, :2 * HEAD_DIM] / den
                b_buf[c, :, blk * 128:(blk + 1) * 128] = o
                ss_b = ss_b + o * o
            yield
        scale_b = lax.rsqrt(jnp.sum(ss_b, axis=-1, keepdims=True) * (1.0 / B_WIDTH) + EPS)
        mix_nx[rows, A_WIDTH:MIX_WIDTH] = (b_buf[c] * scale_b * gb_ref[...]).astype(BF16)
        yield

    def mix_pieces():
        attention = (mix_b(c) for c in range(n_chunks))
        gens = [next(attention), mix_a()]
        while gens:
            for gen in list(gens):
                try:
                    next(gen)
                    yield
                except StopIteration:
                    following = next(attention, None) if gen is gens[0] else None
                    if following is None:
                        gens.remove(gen)
                    else:
                        gens[0] = following

    def output_pieces():
        mix_cu[...] = mix_nx[...]
        sumsq = []
        for c0 in range(0, D_MODEL, MXU_COLS):
            def piece(c0=c0):
                hp = x3_ref[:, c0:c0 + MXU_COLS] + jnp.dot(
                    mix_cu[...], wout_ref[:, c0:c0 + MXU_COLS], preferred_element_type=F32)
                o_ref[:, c0:c0 + MXU_COLS] = hp
                sq = hp * hp
                sumsq.append(sq[:, :MXU_COLS // 2] + sq[:, MXU_COLS // 2:])
            yield piece

        def scale_piece():
            total = functools.reduce(lambda a, b: a + b, sumsq)
            scale = lax.rsqrt(jnp.sum(total, axis=-1, keepdims=True) * (1.0 / D_MODEL) + EPS)
            rs_ref[...] = jnp.broadcast_to(scale, rs_ref.shape)
        yield scale_piece

    def run():
        out_gen = output_pieces()
        next(out_gen)()
        matmuls = list(out_gen)
        take_projection()
        matmuls += list(project_pieces())
        cast_ffn_weights()
        n_yields = n_chunks * MIX_B_PIECES + A_GROUPS * (n_chunks + 1) + n_chunks
        spread = len(matmuls) - TRAILING_MATMULS
        done = 0
        for k, _ in enumerate(mix_pieces(), start=1):
            want = (k * spread) // n_yields
            while done < want:
                matmuls[done]()
                done += 1
        for piece in matmuls[done:]:
            piece()

    @pl.when(step == 0)
    def _():
        row = lax.broadcasted_iota(jnp.int32, (CHUNK, CHUNK), 0)
        col = lax.broadcasted_iota(jnp.int32, (CHUNK, CHUNK), 1)
        for g in range(A_GROUPS):
            ws_bf[g] = jnp.where(row >= col, ws_ref[g], 0.0).astype(BF16)
            diag = jnp.where(row == col, bsp_ref[g:g + 1, :], 0.0)
            bs_ref[g] = jnp.broadcast_to(jnp.sum(diag, axis=1, keepdims=True), (CHUNK, CHUNK))
        for c0 in range(0, D_MODEL, CHUNK):
            diag = jnp.where(row == col, gff_ref[:, c0:c0 + CHUNK], 0.0)
            gcol[c0:c0 + CHUNK, :] = jnp.sum(diag, axis=1, keepdims=True)
        for ref in (kvbuf, u_nx, v_nx, q_nx, kv_nx, mix_nx):
            ref[...] = jnp.zeros_like(ref)
        load_projection_weights()

    run()


def _mix_block(x2, g, w_in, lng, lnb, w_s, b_s, sinks, table, bucket, ga, gb, w_out, w_up,
               w_down, g_ffn, layer, tq, seq):
    t = x2.shape[0]
    n_tiles = t // tq
    n_chunks = tq // CHUNK
    proj = lambda w: pltpu.VMEM((tq, w), BF16)
    out_tile = lambda s: (jnp.clip(s - 2, 0, n_tiles - 1), 0)
    up_rows, dn_rows = D_MODEL // n_tiles, D_FF // n_tiles
    slab = lambda s: (jnp.minimum(s, n_tiles - 1), 0)
    return pl.pallas_call(
        functools.partial(_mix_block_kernel, layer, seq // CHUNK),
        out_shape=[jax.ShapeDtypeStruct((t, D_MODEL), F32),
                   jax.ShapeDtypeStruct((t, RS_LANES), F32),
                   jax.ShapeDtypeStruct((D_MODEL, D_FF), BF16),
                   jax.ShapeDtypeStruct((D_FF, D_MODEL), BF16)],
        grid=(n_tiles + 2,),
        in_specs=[
            pl.BlockSpec((tq, D_MODEL), lambda s: (jnp.minimum(s, n_tiles - 1), 0)),
            pl.BlockSpec((tq, D_MODEL), out_tile),
            _const_spec((1, D_MODEL)),
            pl.BlockSpec(memory_space=pl.ANY),
            _const_spec((1, A_WIDTH)),
            _const_spec((1, A_WIDTH)),
            _const_spec(w_s.shape),
            _const_spec(b_s.shape),
            pl.BlockSpec(memory_space=pltpu.SMEM),
            pl.BlockSpec(memory_space=pltpu.SMEM),
            _const_spec(bucket.shape),
            _const_spec((1, A_WIDTH)),
            _const_spec((1, B_WIDTH)),
            pl.BlockSpec(memory_space=pl.ANY),
            pl.BlockSpec((None, up_rows, D_FF), lambda s: (layer,) + slab(s)),
            pl.BlockSpec((None, dn_rows, D_MODEL), lambda s: (layer,) + slab(s)),
            _const_spec((1, D_MODEL)),
        ],
        out_specs=[pl.BlockSpec((tq, D_MODEL), out_tile),
                   pl.BlockSpec((tq, RS_LANES), out_tile),
                   pl.BlockSpec((up_rows, D_FF), slab),
                   pl.BlockSpec((dn_rows, D_MODEL), slab)],
        scratch_shapes=[
            pltpu.VMEM((A_GROUPS, CHUNK, CHUNK), BF16),
            pltpu.VMEM((A_GROUPS, CHUNK, CHUNK), F32),
            pltpu.VMEM((D_MODEL, 1), F32),
            pltpu.VMEM((2, B_KV_HEADS, PAIRS_PER_KV, 2, CHUNK, 2 * CHUNK), F32),
            pltpu.VMEM((D_MODEL, w_in.shape[2]), BF16),
            pltpu.VMEM((MIX_WIDTH, D_MODEL), BF16),
            pltpu.VMEM((2, WEIGHT_STAGE_ROWS, w_in.shape[2]), F32),
            pltpu.VMEM((2, WEIGHT_STAGE_ROWS, D_MODEL), F32),
            pltpu.SemaphoreType.DMA((2, 2)),
            proj(D_MODEL),
            proj(A_WIDTH), proj(A_WIDTH), proj(B_WIDTH), proj(2 * KV_WIDTH),
            proj(A_WIDTH), proj(A_WIDTH), proj(B_WIDTH),
            pltpu.VMEM((tq + CHUNK, 2 * KV_WIDTH), BF16),
            proj(MIX_WIDTH), proj(MIX_WIDTH),
            pltpu.VMEM((n_chunks, CHUNK, A_WIDTH), F32),
            pltpu.VMEM((n_chunks, CHUNK, B_WIDTH), F32),
        ],
        compiler_params=_params(1),
        name="mix_block",
    )(x2, x2, g, w_in, lng, lnb, w_s, b_s, sinks, table, bucket, ga, gb, w_out, w_up, w_down,
      g_ffn)


def _ffn_kernel(final_norm, h_hbm, rs_hbm, wu_ref, wd_ref, gf_ref, o_hbm,
                acc, nbuf, rs_stage, ssq, sem_in, sem_out):
    i = pl.program_id(0)
    f = pl.program_id(1)
    n_tiles = pl.num_programs(0)
    last_f = pl.num_programs(1) - 1
    tm = acc.shape[1]
    n_row_chunks = tm // FFN_NORM_ROWS
    slot = i % 2
    other = 1 - slot

    def h_copies(tile, s):
        rows = pl.ds(tile * tm, tm)
        return (pltpu.make_async_copy(h_hbm.at[rows], acc.at[s], sem_in.at[0]),
                pltpu.make_async_copy(rs_hbm.at[rows], rs_stage, sem_in.at[1]))

    def o_copy(tile, s):
        return pltpu.make_async_copy(acc.at[s], o_hbm.at[pl.ds(tile * tm, tm)], sem_out.at[s])

    def chunk(k):
        return pl.ds(pl.multiple_of(k * FFN_NORM_ROWS, FFN_NORM_ROWS), FFN_NORM_ROWS)

    @pl.when(f == 0)
    def _():
        @pl.when(i == 0)
        def _():
            for cp in h_copies(0, 0):
                cp.start()
        for cp in h_copies(i, slot):
            cp.wait()

        def norm_rows(k, carry):
            scale = rs_stage[chunk(k), 0:1]
            nbuf[chunk(k), :] = (acc[slot, chunk(k), :] * scale).astype(BF16)
            return carry
        lax.fori_loop(0, n_row_chunks, norm_rows, 0, unroll=True)

    @pl.when((f == 1) & (i + 1 < n_tiles))
    def _():
        @pl.when(i >= 1)
        def _():
            o_copy(i - 1, other).wait()
        for cp in h_copies(i + 1, other):
            cp.start()

    z = jnp.maximum(jnp.dot(nbuf[...], wu_ref[...], preferred_element_type=F32), 0.0)
    update = jnp.dot((z * z).astype(BF16), wd_ref[...], preferred_element_type=F32)
    y = acc[slot] + update
    acc[slot] = y
    if final_norm:
        ssq[...] = jnp.sum(y * y, axis=-1, keepdims=True)

    @pl.when(f == last_f)
    def _():
        if final_norm:
            def final_rows(k, carry):
                scale = lax.rsqrt(ssq[chunk(k), :] * (1.0 / D_MODEL) + EPS)
                acc[slot, chunk(k), :] = acc[slot, chunk(k), :] * scale * gf_ref[...]
                return carry
            lax.fori_loop(0, n_row_chunks, final_rows, 0, unroll=True)
        o_copy(i, slot).start()

        @pl.when(i == n_tiles - 1)
        def _():
            @pl.when(i >= 1)
            def _():
                o_copy(i - 1, other).wait()
            o_copy(i, slot).wait()


def _ffn(h, rs, w_up, w_down, gf, final_norm, tm, tf):
    t = h.shape[0]
    assert D_FF // tf >= 2 and tm % FFN_NORM_ROWS == 0
    return pl.pallas_call(
        functools.partial(_ffn_kernel, final_norm),
        out_shape=jax.ShapeDtypeStruct((t, D_MODEL), F32),
        grid=(t // tm, D_FF // tf),
        in_specs=[
            pl.BlockSpec(memory_space=pl.ANY),
            pl.BlockSpec(memory_space=pl.ANY),
            pl.BlockSpec((D_MODEL, tf), lambda i, f: (0, f)),
            pl.BlockSpec((tf, D_MODEL), lambda i, f: (f, 0)),
            _const_spec((1, D_MODEL)),
        ],
        out_specs=pl.BlockSpec(memory_space=pl.ANY),
        scratch_shapes=[
            pltpu.VMEM((2, tm, D_MODEL), F32),
            pltpu.VMEM((tm, D_MODEL), BF16),
            pltpu.VMEM((tm, RS_LANES), F32),
            pltpu.VMEM((tm, 1), F32),
            pltpu.SemaphoreType.DMA((2,)),
            pltpu.SemaphoreType.DMA((2,)),
        ],
        compiler_params=_params(2),
        name="ffn",
    )(h, rs, w_up, w_down, gf)


def _relative_buckets():
    i = jnp.arange(CHUNK)[:, None]
    j = jnp.arange(2 * CHUNK)[None, :]
    rel = i + CHUNK - j
    relc = jnp.maximum(rel, 0)
    n_exact = N_BUCKETS // 2
    relf = jnp.maximum(relc, n_exact).astype(F32)
    large = n_exact + (jnp.log(relf / n_exact) / math.log(MAX_DISTANCE / n_exact)
                       * (N_BUCKETS - n_exact)).astype(jnp.int32)
    large = jnp.minimum(large, N_BUCKETS - 1)
    bucket = jnp.where(relc < n_exact, relc, large)
    return jnp.where((rel >= 0) & (rel < CHUNK), bucket, -1).astype(jnp.int32)


def kernel(x, rel_bias_table, mix_norm_g, w_in, gate_norm_g, gate_norm_b, w_spatial, b_spatial,
           attn_sinks, out_norm_a_g, out_norm_b_g, w_out, ffn_norm_g, w_up, w_down, final_norm_g):
    bsz, seq, d = x.shape
    depth = w_in.shape[0]
    t = bsz * seq
    tq, tm = 2 * CHUNK, 1024
    assert d == D_MODEL and x.dtype == F32, (x.shape, x.dtype)
    assert seq % tq == 0 and t % tm == 0, (bsz, seq)
    assert w_in.shape[1:] == (D_MODEL, 2 * A_WIDTH + B_WIDTH + 2 * KV_WIDTH), w_in.shape
    assert w_out.shape[1:] == (MIX_WIDTH, D_MODEL) and w_up.shape[1:] == (D_MODEL, D_FF)
    assert w_down.shape[1:] == (D_FF, D_MODEL) and w_spatial.shape[1:] == (A_GROUPS, CHUNK, CHUNK)
    assert rel_bias_table.shape == (N_BUCKETS, B_HEADS) and attn_sinks.shape[1:] == (B_HEADS,)
    bucket = _relative_buckets()
    h = x.reshape(t, d)
    for layer in range(depth):
        h_mid, h_mid_rs, w_up_bf, w_down_bf = _mix_block(
            h, mix_norm_g[layer].reshape(1, d), w_in,
            gate_norm_g[layer].reshape(1, A_WIDTH), gate_norm_b[layer].reshape(1, A_WIDTH),
            w_spatial[layer], b_spatial[layer], attn_sinks[layer],
            rel_bias_table.astype(F32), bucket,
            out_norm_a_g[layer].reshape(1, A_WIDTH), out_norm_b_g[layer].reshape(1, B_WIDTH),
            w_out, w_up, w_down, ffn_norm_g[layer].reshape(1, d), layer=layer, tq=tq, seq=seq)
        h = _ffn(h_mid, h_mid_rs, w_up_bf, w_down_bf,
                 final_norm_g.reshape(1, d), final_norm=layer == depth - 1, tm=tm, tf=2048)
    return h.reshape(bsz, seq, d)
```
